```python
import math
import jax, jax.numpy as jnp
from jax import lax
import numpy as np

D_MODEL = 1024
BATCH = 16
SEQ = 2048
DEPTH = 1

CHUNK = 64
Q_BLOCK = 128
NORM_EPS = 1e-6

DA_HEADS = 4
DA_QK_DIM = 64
DA_V_DIM = 2 * DA_QK_DIM
DA_WIDTH = DA_HEADS * DA_V_DIM
ALIBI_SLOPES = tuple(2.0 ** (-8.0 * (h + 1) / DA_HEADS) for h in range(DA_HEADS))

RET_HEADS = 4
RET_QK_DIM = 64
RET_V_DIM = 128
RET_WIDTH = RET_HEADS * RET_V_DIM
RET_GAMMA = tuple(1.0 - 2.0 ** (-5.0 - h) for h in range(RET_HEADS))

MIX_WIDTH = DA_WIDTH + RET_WIDTH

DA_Q_COLS = DA_HEADS * 2 * DA_QK_DIM
DA_K_COLS = DA_HEADS * 2 * DA_QK_DIM
DA_V_COLS = DA_WIDTH
RET_Q_COLS = RET_HEADS * RET_QK_DIM
RET_K_COLS = RET_HEADS * RET_QK_DIM
RET_V_COLS = RET_WIDTH
RET_G_COLS = RET_WIDTH
IN_SPLITS = tuple(np.cumsum([DA_Q_COLS, DA_K_COLS, DA_V_COLS, RET_Q_COLS, RET_K_COLS, RET_V_COLS]).tolist())
IN_WIDTH = DA_Q_COLS + DA_K_COLS + DA_V_COLS + RET_Q_COLS + RET_K_COLS + RET_V_COLS + RET_G_COLS

D_FF = -(-8 * D_MODEL // (3 * 256)) * 256

kernel_name = "hybrid_diffattn_retention_swiglu_sandwich"


def rmsnorm(x, g):
    x32 = x.astype(jnp.float32)
    y = x32 * lax.rsqrt(jnp.mean(x32 * x32, axis=-1, keepdims=True) + NORM_EPS)
    return (y * g.astype(jnp.float32)).astype(x.dtype)


def lambda_init_for(layer_idx):
    return 0.8 - 0.6 * math.exp(-0.3 * layer_idx)


def diff_attention(q, k, v, lam, subln_g, lambda_init):
    B, S = q.shape[0], q.shape[1]
    scale = DA_QK_DIM ** -0.5
    slopes = jnp.asarray(ALIBI_SLOPES, jnp.float32)
    lam32 = lam.astype(jnp.float32)
    outs = []
    for start in range(0, S, Q_BLOCK):
        end = start + Q_BLOCK
        qb = q[:, start:end]
        kb = k[:, :end]
        vb = v[:, :end]
        s = jnp.einsum('bqhmd,bkhmd->bhmqk', qb, kb).astype(jnp.float32) * scale
        qpos = jnp.arange(start, end)
        kpos = jnp.arange(end)
        dist = jnp.abs(qpos[:, None] - kpos[None, :]).astype(jnp.float32)
        bias = -slopes[:, None, None] * dist
        allowed = (kpos[None, :] // CHUNK) <= (qpos[:, None] // CHUNK)
        s = jnp.where(allowed, s + bias[None, :, None], -jnp.inf)
        p = jax.nn.softmax(s, axis=-1)
        pd = p[:, :, 0] - lam32 * p[:, :, 1]
        outs.append(jnp.einsum('bhqk,bkhd->bqhd', pd.astype(vb.dtype), vb))
    o = jnp.concatenate(outs, axis=1)
    o = rmsnorm(o, subln_g) * jnp.asarray(1.0 - lambda_init, o.dtype)
    return o.reshape(B, S, DA_WIDTH)


def retention(q, k, v, gate, norm_g):
    B, S, H, dk = q.shape
    dv = v.shape[-1]
    nc = S // CHUNK
    k = k * jnp.asarray(dk ** -0.5, k.dtype)
    log_g = jnp.log(jnp.asarray(RET_GAMMA, jnp.float32))
    i = jnp.arange(CHUNK, dtype=jnp.float32)
    intra = jnp.exp(log_g[:, None, None] * jnp.abs(i[:, None] - i[None, :]))
    q_decay = jnp.exp(log_g[:, None] * (i + 1.0))
    k_decay = jnp.exp(log_g[:, None] * (CHUNK - 1.0 - i))
    chunk_decay = jnp.exp(log_g * CHUNK)

    def to_chunks(t):
        return t.reshape(B, nc, CHUNK, H, t.shape[-1]).transpose(1, 0, 3, 2, 4)

    qc, kc, vc = to_chunks(q), to_chunks(k), to_chunks(v)

    def step(state, qkv):
        qi, ki, vi = qkv
        s = jnp.einsum('bhid,bhjd->bhij', qi, ki).astype(jnp.float32) * intra
        inner = jnp.einsum('bhij,bhjv->bhiv', s, vi.astype(jnp.float32))
        cross = jnp.einsum('bhid,bhdv->bhiv', qi.astype(jnp.float32) * q_decay[..., None], state)
        new_state = state * chunk_decay[:, None, None] + jnp.einsum(
            'bhjd,bhjv->bhdv', ki.astype(jnp.float32) * k_decay[..., None], vi.astype(jnp.float32))
        return new_state, inner + cross

    state0 = jnp.zeros((B, H, dk, dv), jnp.float32)
    _, o = lax.scan(step, state0, (qc, kc, vc))
    o = o.transpose(1, 0, 3, 2, 4).reshape(B, S, H, dv).astype(v.dtype)
    o = rmsnorm(o, norm_g)
    o = jax.nn.silu(gate) * o
    return o.reshape(B, S, RET_WIDTH)


def setup_inputs(seed: int = 0) -> dict:
    key = jax.random.key(seed)
    ks = jax.random.split(key, 20)
    f32 = jnp.float32

    def nrm(k, shape, scale):
        return jax.random.normal(k, shape, f32) * scale

    def gain(k, shape):
        return 1.0 + 0.02 * jax.random.normal(k, shape, f32)

    return {
        "x": jax.random.normal(ks[0], (BATCH, SEQ, D_MODEL), f32),
        "pre_mix_g": gain(ks[1], (DEPTH, D_MODEL)),
        "w_in": nrm(ks[2], (DEPTH, D_MODEL, IN_WIDTH), D_MODEL ** -0.5),
        "lambda_q1": nrm(ks[3], (DEPTH, DA_QK_DIM), 0.1),
        "lambda_k1": nrm(ks[4], (DEPTH, DA_QK_DIM), 0.1),
        "lambda_q2": nrm(ks[5], (DEPTH, DA_QK_DIM), 0.1),
        "lambda_k2": nrm(ks[6], (DEPTH, DA_QK_DIM), 0.1),
        "da_subln_g": gain(ks[7], (DEPTH, DA_V_DIM)),
        "ret_norm_g": gain(ks[8], (DEPTH, RET_V_DIM)),
        "w_out": nrm(ks[9], (DEPTH, MIX_WIDTH, D_MODEL), MIX_WIDTH ** -0.5),
        "post_mix_g": gain(ks[10], (DEPTH, D_MODEL)),
        "pre_ffn_g": gain(ks[11], (DEPTH, D_MODEL)),
        "w_gate": nrm(ks[12], (DEPTH, D_MODEL, D_FF), D_MODEL ** -0.5),
        "w_up": nrm(ks[13], (DEPTH, D_MODEL, D_FF), D_MODEL ** -0.5),
        "w_down": nrm(ks[14], (DEPTH, D_FF, D_MODEL), D_FF ** -0.5),
        "post_ffn_g": gain(ks[15], (DEPTH, D_MODEL)),
    }


def reference(x, pre_mix_g, w_in, lambda_q1, lambda_k1, lambda_q2, lambda_k2, da_subln_g,
              ret_norm_g, w_out, post_mix_g, pre_ffn_g, w_gate, w_up, w_down, post_ffn_g):
    B, S = x.shape[0], x.shape[1]
    for l in range(DEPTH):
        h = rmsnorm(x, pre_mix_g[l])
        proj = jnp.einsum('bsd,de->bse', h, w_in[l])
        da_q, da_k, da_v, r_q, r_k, r_v, r_g = jnp.split(proj, IN_SPLITS, axis=-1)
        da_q = da_q.reshape(B, S, DA_HEADS, 2, DA_QK_DIM)
        da_k = da_k.reshape(B, S, DA_HEADS, 2, DA_QK_DIM)
        da_v = da_v.reshape(B, S, DA_HEADS, DA_V_DIM)
        lam_init = lambda_init_for(l)
        lam = (jnp.exp(jnp.sum(lambda_q1[l].astype(jnp.float32) * lambda_k1[l].astype(jnp.float32)))
               - jnp.exp(jnp.sum(lambda_q2[l].astype(jnp.float32) * lambda_k2[l].astype(jnp.float32)))
               + lam_init)
        o_da = diff_attention(da_q, da_k, da_v, lam, da_subln_g[l], lam_init)
        o_ret = retention(r_q.reshape(B, S, RET_HEADS, RET_QK_DIM),
                          r_k.reshape(B, S, RET_HEADS, RET_QK_DIM),
                          r_v.reshape(B, S, RET_HEADS, RET_V_DIM),
                          r_g.reshape(B, S, RET_HEADS, RET_V_DIM),
                          ret_norm_g[l])
        mix = jnp.einsum('bse,ed->bsd', jnp.concatenate([o_da, o_ret], axis=-1), w_out[l])
        x = x + rmsnorm(mix, post_mix_g[l])
        h = rmsnorm(x, pre_ffn_g[l])
        f = jax.nn.silu(jnp.einsum('bsd,df->bsf', h, w_gate[l])) * jnp.einsum('bsd,df->bsf', h, w_up[l])
        f = jnp.einsum('bsf,fd->bsd', f, w_down[l])
        x = x + rmsnorm(f, post_ffn_g[l])
    return x
```

```python
import functools
import math

import numpy as np
import jax
import jax.numpy as jnp
from jax import lax
from jax.experimental import pallas as pl
from jax.experimental.pallas import tpu as pltpu

F32 = jnp.float32
BF16 = jnp.bfloat16

D_MODEL = 1024
CHUNK = 64
NORM_EPS = 1e-6
DA_HEADS = 4
DA_QK_DIM = 64
DA_V_DIM = 128
DA_WIDTH = DA_HEADS * DA_V_DIM
ALIBI_SLOPES = tuple(2.0 ** (-8.0 * (h + 1) / DA_HEADS) for h in range(DA_HEADS))
RET_HEADS = 4
RET_QK_DIM = 64
RET_V_DIM = 128
RET_WIDTH = RET_HEADS * RET_V_DIM
RET_GAMMA = tuple(1.0 - 2.0 ** (-5.0 - h) for h in range(RET_HEADS))
DA_Q_COLS = DA_HEADS * 2 * DA_QK_DIM
DA_K_COLS = DA_HEADS * 2 * DA_QK_DIM
DA_V_COLS = DA_WIDTH
RET_Q_COLS = RET_HEADS * RET_QK_DIM
RET_K_COLS = RET_HEADS * RET_QK_DIM

V7X_LANES = 128
V7X_VMEM_BYTES = 64 * 1024 * 1024
VMEM_LIMIT_BYTES = V7X_VMEM_BYTES - 8 * 1024 * 1024

ROW_TILE = 512
ATT_TILE = 256
RET_BLOCK = 256
LOG2E = math.log2(math.e)
POS_TERMS = 3

_NT = (((1,), (1,)), ((), ()))
_TN = (((0,), (0,)), ((), ()))


def _rms(x):
    return x * lax.rsqrt(jnp.mean(x * x, axis=-1, keepdims=True) + NORM_EPS)


def _const_spec(shape):
    nd = len(shape)
    return pl.BlockSpec(shape, lambda *_: (0,) * nd, pipeline_mode=pl.Buffered(1))


def _inproj_kernel(x_ref, g_ref, wqT_ref, wk_ref, wvT_ref, wr_ref,
                   qT_ref, k_ref, vT_ref, rq_ref, rk_ref, rv_ref, rg_ref):
    h = (_rms(x_ref[...]) * g_ref[...]).astype(BF16)
    qT = lax.dot_general(wqT_ref[...], h, _NT, preferred_element_type=F32)
    qT_ref[...] = (qT * (DA_QK_DIM ** -0.5 * LOG2E)).astype(BF16)
    k = jnp.dot(h, wk_ref[...], preferred_element_type=F32)
    for hh in range(DA_HEADS):
        k_ref[hh] = k[:, hh * 128:(hh + 1) * 128].astype(BF16)
    vT = lax.dot_general(wvT_ref[...], h, _NT, preferred_element_type=F32)
    for j in range(ROW_TILE // ATT_TILE):
        vT_ref[j] = vT[:, j * ATT_TILE:(j + 1) * ATT_TILE].astype(BF16)
    r = jnp.dot(h, wr_ref[...], preferred_element_type=F32)
    rq_ref[...] = r[:, 0:256].astype(BF16)
    rk_ref[...] = r[:, 256:512].astype(BF16)
    rv_ref[...] = r[:, 512:1024].astype(BF16)
    rg_ref[...] = r[:, 1024:1536].astype(BF16)


def _inproj(x2, g, wqT, wk, wvT, wr):
    n = x2.shape[0]
    tm = ROW_TILE
    row = lambda i: (i, 0)
    out_shapes = (
        jax.ShapeDtypeStruct((DA_Q_COLS, n), BF16),
        jax.ShapeDtypeStruct((DA_HEADS, n, 128), BF16),
        jax.ShapeDtypeStruct((n // ATT_TILE, DA_V_COLS, ATT_TILE), BF16),
        jax.ShapeDtypeStruct((n, RET_Q_COLS), BF16),
        jax.ShapeDtypeStruct((n, RET_K_COLS), BF16),
        jax.ShapeDtypeStruct((n, RET_WIDTH), BF16),
        jax.ShapeDtypeStruct((n, RET_WIDTH), BF16),
    )
    out_specs = (
        pl.BlockSpec((DA_Q_COLS, tm), lambda i: (0, i)),
        pl.BlockSpec((DA_HEADS, tm, 128), lambda i: (0, i, 0)),
        pl.BlockSpec((tm // ATT_TILE, DA_V_COLS, ATT_TILE), lambda i: (i, 0, 0)),
        pl.BlockSpec((tm, RET_Q_COLS), row),
        pl.BlockSpec((tm, RET_K_COLS), row),
        pl.BlockSpec((tm, RET_WIDTH), row),
        pl.BlockSpec((tm, RET_WIDTH), row),
    )
    return pl.pallas_call(
        _inproj_kernel,
        grid=(n // tm,),
        in_specs=[
            pl.BlockSpec((tm, D_MODEL), row),
            _const_spec((1, D_MODEL)),
            _const_spec(wqT.shape),
            _const_spec(wk.shape),
            _const_spec(wvT.shape),
            _const_spec(wr.shape),
        ],
        out_specs=out_specs,
        out_shape=out_shapes,
        compiler_params=pltpu.CompilerParams(
            dimension_semantics=("arbitrary",), vmem_limit_bytes=VMEM_LIMIT_BYTES),
        name="inproj",
    )(x2, g, wqT, wk, wvT, wr)


def _attn_tables():
    t = ATT_TILE
    kk = np.arange(t)[:, None]
    qq = np.arange(t)[None, :]
    allowed = (kk // CHUNK) <= (qq // CHUNK)
    diag = np.empty((DA_HEADS, t, 2 * t), np.float32)
    prow = np.zeros((DA_HEADS, 128, 2 * t), np.float32)
    for h, slope in enumerate(ALIBI_SLOPES):
        c = slope * LOG2E
        d = np.where(allowed, -2.0 * c * np.maximum(kk - qq, 0), -np.inf)
        diag[h] = np.concatenate([d, d], axis=1)
        rem = c
        for j in range(POS_TERMS):
            piece = float(np.asarray(rem, np.float32).astype(jnp.bfloat16).astype(np.float32))
            prow[h, j, :] = piece * CHUNK
            prow[h, POS_TERMS + j, :] = piece
            rem = rem - piece
    return diag, prow


def _pos_features(s):
    pos = np.arange(s)
    feat = np.zeros((s, 128), np.float32)
    for j in range(POS_TERMS):
        feat[:, j] = pos // CHUNK
        feat[:, POS_TERMS + j] = pos % CHUNK
    return feat


def _attn_kernel(lam_init, qT_ref, k_ref, pos_ref, vT_ref, diag_ref, prow_ref, lamv_ref, g_ref,
                 o_ref, acc_ref):
    t = ATT_TILE
    qi = pl.program_id(2)
    qT = qT_ref[...]
    z = jnp.zeros((DA_QK_DIM, t), BF16)
    q01 = jnp.concatenate([jnp.concatenate([qT[:DA_QK_DIM], z], axis=0),
                           jnp.concatenate([z, qT[DA_QK_DIM:]], axis=0)], axis=1)
    qext = jnp.concatenate([q01, prow_ref[0]], axis=0)
    acc_ref[...] = jnp.zeros_like(acc_ref)

    def tile(ki, m, l, on_diagonal):
        rows = pl.ds(pl.multiple_of(ki * t, t), t)
        kext = jnp.concatenate([k_ref[rows, :], pos_ref[rows, :]], axis=1)
        s = jnp.dot(kext, qext, preferred_element_type=F32)
        if on_diagonal:
            s = s + diag_ref[0]
        m_new = jnp.maximum(m, jnp.max(s, axis=0, keepdims=True))
        alpha = jnp.exp2(m - m_new)
        p = jnp.exp2(s - m_new)
        l_new = alpha * l + jnp.sum(p, axis=0, keepdims=True)
        pv = jnp.dot(vT_ref[ki], p.astype(BF16), preferred_element_type=F32)
        acc_ref[...] = acc_ref[...] * alpha + pv
        return m_new, l_new

    m0 = jnp.full((1, 2 * t), -jnp.inf, F32)
    l0 = jnp.zeros((1, 2 * t), F32)
    m, l = lax.fori_loop(0, qi, lambda ki, c: tile(ki, c[0], c[1], False), (m0, l0))
    m, l = tile(qi, m, l, True)

    lv = lamv_ref[...]
    lam = (jnp.exp(jnp.sum(lv[0:1] * lv[1:2], axis=-1, keepdims=True))
           - jnp.exp(jnp.sum(lv[2:3] * lv[3:4], axis=-1, keepdims=True)) + lam_init)
    acc = acc_ref[...]
    inv = 1.0 / l
    o = acc[:, :t] * inv[:, :t] - lam * (acc[:, t:] * inv[:, t:])
    o = o * lax.rsqrt(jnp.mean(o * o, axis=0, keepdims=True) + NORM_EPS)
    o_ref[...] = (o.T * g_ref[...] * (1.0 - lam_init)).astype(o_ref.dtype)


def _attention(qT, k, vT, lamv, subln_g, batch, seq, lam_init):
    t = ATT_TILE
    nq = seq // t
    n = batch * seq
    diag, prow = _attn_tables()
    diag = jnp.asarray(diag)
    prow = jnp.asarray(prow, BF16)
    pos = jnp.asarray(_pos_features(seq), BF16)
    return pl.pallas_call(
        functools.partial(_attn_kernel, lam_init),
        grid=(batch, DA_HEADS, nq),
        in_specs=[
            pl.BlockSpec((128, t), lambda b, h, i: (h, b * nq + i)),
            pl.BlockSpec((None, seq, 128), lambda b, h, i: (h, b, 0)),
            pl.BlockSpec((seq, 128), lambda b, h, i: (0, 0)),
            pl.BlockSpec((nq, 128, t), lambda b, h, i: (b, h, 0)),
            pl.BlockSpec((1, t, 2 * t), lambda b, h, i: (h, 0, 0)),
            pl.BlockSpec((1, 128, 2 * t), lambda b, h, i: (h, 0, 0)),
            pl.BlockSpec((4, DA_QK_DIM), lambda b, h, i: (0, 0)),
            pl.BlockSpec((1, DA_V_DIM), lambda b, h, i: (0, 0)),
        ],
        out_specs=pl.BlockSpec((t, DA_V_DIM), lambda b, h, i: (b * nq + i, h)),
        out_shape=jax.ShapeDtypeStruct((n, DA_WIDTH), BF16),
        scratch_shapes=[pltpu.VMEM((DA_V_DIM, 2 * t), F32)],
        compiler_params=pltpu.CompilerParams(
            dimension_semantics=("arbitrary", "arbitrary", "arbitrary"),
            vmem_limit_bytes=VMEM_LIMIT_BYTES),
        name="diff_attention",
    )(qT, k, pos, vT, diag, prow, lamv, subln_g)


def _ret_tables():
    L = RET_BLOCK
    i = np.arange(L)[:, None].astype(np.float64)
    j = np.arange(L)[None, :].astype(np.float64)
    allowed = (j // CHUNK) <= (i // CHUNK)
    dmat = np.empty((RET_HEADS, L, L), np.float32)
    qdec = np.zeros((RET_HEADS, L, 128), np.float32)
    kdec = np.zeros((RET_HEADS, L, 128), np.float32)
    cdec = np.empty((RET_HEADS, 1, 128), np.float32)
    lane = np.arange(128)[None, :]
    for h, gamma in enumerate(RET_GAMMA):
        mine = (lane // RET_QK_DIM) == (h % 2)
        dmat[h] = np.where(allowed, gamma ** np.abs(i - j), 0.0) * RET_QK_DIM ** -0.5
        qdec[h] = np.where(mine, gamma ** (i + 1.0), 0.0)
        kdec[h] = np.where(mine, gamma ** (L - 1.0 - i), 0.0) * RET_QK_DIM ** -0.5
        cdec[h] = gamma ** L
    return dmat, qdec, kdec, cdec


def _ret_kernel(q_ref, k_ref, v_ref, gate_ref, dmat_ref, qdec_ref, kdec_ref, cdec_ref, g_ref,
                o_ref, state_ref):
    @pl.when(pl.program_id(2) == 0)
    def _():
        state_ref[...] = jnp.zeros_like(state_ref)

    q = q_ref[...].astype(F32)
    k = k_ref[...]
    v = v_ref[...]
    qdec = qdec_ref[0]
    qm = jnp.where(qdec > 0.0, q, 0.0).astype(BF16)
    s = lax.dot_general(qm, k, _NT, preferred_element_type=F32) * dmat_ref[0]
    inner = jnp.dot(s.astype(BF16), v, preferred_element_type=F32)
    state = state_ref[...]
    cross = jnp.dot((q * qdec).astype(BF16), state.astype(BF16), preferred_element_type=F32)
    kd = (k.astype(F32) * kdec_ref[0]).astype(BF16)
    state_ref[...] = state * cdec_ref[0] + lax.dot_general(kd, v, _TN, preferred_element_type=F32)
    o = _rms(inner + cross) * g_ref[...]
    gate = gate_ref[...].astype(F32)
    o_ref[...] = (gate * (1.0 / (1.0 + jnp.exp(-gate))) * o).astype(o_ref.dtype)


def _retention(rq, rk, rv, rg, norm_g, batch, seq):
    L = RET_BLOCK
    nb = seq // L
    n = batch * seq
    dmat, qdec, kdec, cdec = (jnp.asarray(a) for a in _ret_tables())
    pair = lambda b, h, i: (b * nb + i, h // 2)
    head = lambda b, h, i: (b * nb + i, h)
    table = lambda b, h, i: (h, 0, 0)
    return pl.pallas_call(
        _ret_kernel,
        grid=(batch, RET_HEADS, nb),
        in_specs=[
            pl.BlockSpec((L, 128), pair),
            pl.BlockSpec((L, 128), pair),
            pl.BlockSpec((L, RET_V_DIM), head),
            pl.BlockSpec((L, RET_V_DIM), head),
            pl.BlockSpec((1, L, L), table),
            pl.BlockSpec((1, L, 128), table),
            pl.BlockSpec((1, L, 128), table),
            pl.BlockSpec((1, 1, 128), table),
            pl.BlockSpec((1, RET_V_DIM), lambda b, h, i: (0, 0)),
        ],
        out_specs=pl.BlockSpec((L, RET_V_DIM), head),
        out_shape=jax.ShapeDtypeStruct((n, RET_WIDTH), BF16),
        scratch_shapes=[pltpu.VMEM((128, RET_V_DIM), F32)],
        compiler_params=pltpu.CompilerParams(
            dimension_semantics=("arbitrary", "arbitrary", "arbitrary"),
            vmem_limit_bytes=VMEM_LIMIT_BYTES),
        name="retention",
    )(rq, rk, rv, rg, dmat, qdec, kdec, cdec, norm_g)


def _ffn_kernel(x_ref, oa_ref, or_ref, wo_ref, g1_ref, g2_ref, wg_ref, wu_ref, wd_ref, g3_ref,
                out_ref):
    mix = (jnp.dot(oa_ref[...], wo_ref[:DA_WIDTH, :], preferred_element_type=F32)
           + jnp.dot(or_ref[...], wo_ref[DA_WIDTH:, :], preferred_element_type=F32))
    x1 = x_ref[...] + _rms(mix) * g1_ref[...]
    h = (_rms(x1) * g2_ref[...]).astype(BF16)
    gate = jnp.dot(h, wg_ref[...], preferred_element_type=F32)
    up = jnp.dot(h, wu_ref[...], preferred_element_type=F32)
    f = (gate * (1.0 / (1.0 + jnp.exp(-gate))) * up).astype(BF16)
    y = jnp.dot(f, wd_ref[...], preferred_element_type=F32)
    out_ref[...] = x1 + _rms(y) * g3_ref[...]


def _ffn(x2, oa, orr, wo, g1, g2, wg, wu, wd, g3):
    n = x2.shape[0]
    tm = ROW_TILE
    row = lambda i: (i, 0)
    return pl.pallas_call(
        _ffn_kernel,
        grid=(n // tm,),
        in_specs=[
            pl.BlockSpec((tm, D_MODEL), row),
            pl.BlockSpec((tm, DA_WIDTH), row),
            pl.BlockSpec((tm, RET_WIDTH), row),
            _const_spec(wo.shape),
            _const_spec((1, D_MODEL)),
            _const_spec((1, D_MODEL)),
            _const_spec(wg.shape),
            _const_spec(wu.shape),
            _const_spec(wd.shape),
            _const_spec((1, D_MODEL)),
        ],
        out_specs=pl.BlockSpec((tm, D_MODEL), row),
        out_shape=jax.ShapeDtypeStruct((n, D_MODEL), F32),
        compiler_params=pltpu.CompilerParams(
            dimension_semantics=("arbitrary",), vmem_limit_bytes=VMEM_LIMIT_BYTES),
        name="outproj_ffn",
    )(x2, oa, orr, wo, g1, g2, wg, wu, wd, g3)


def kernel(x, pre_mix_g, w_in, lambda_q1, lambda_k1, lambda_q2, lambda_k2, da_subln_g, ret_norm_g,
           w_out, post_mix_g, pre_ffn_g, w_gate, w_up, w_down, post_ffn_g):
    batch, seq, d = x.shape
    depth = w_in.shape[0]
    assert d == D_MODEL and seq % ATT_TILE == 0 and seq % RET_BLOCK == 0
    assert (batch * seq) % ROW_TILE == 0 and ROW_TILE % ATT_TILE == 0
    x2 = x.reshape(batch * seq, d)
    c1 = DA_Q_COLS
    c2 = c1 + DA_K_COLS
    c3 = c2 + DA_V_COLS
    for l in range(depth):
        w = w_in[l].astype(BF16)
        qT, k, vT, rq, rk, rv, rg = _inproj(
            x2, pre_mix_g[l][None, :], w[:, :c1].T, w[:, c1:c2], w[:, c2:c3].T, w[:, c3:])
        lam_init = 0.8 - 0.6 * math.exp(-0.3 * l)
        lamv = jnp.stack([lambda_q1[l], lambda_k1[l], lambda_q2[l], lambda_k2[l]]).astype(F32)
        o_da = _attention(qT, k, vT, lamv, da_subln_g[l][None, :].astype(F32), batch, seq, lam_init)
        o_ret = _retention(rq, rk, rv, rg, ret_norm_g[l][None, :].astype(F32), batch, seq)
        x2 = _ffn(x2, o_da, o_ret, w_out[l].astype(BF16), post_mix_g[l][None, :],
                  pre_ffn_g[l][None, :], w_gate[l].astype(BF16), w_up[l].astype(BF16),
                  w_down[l].astype(BF16), post_ffn_g[l][None, :])
    return x2.reshape(batch, seq, d)
```

```python
import functools
import math

import numpy as np
import jax
import jax.numpy as jnp
from jax import lax
from jax.experimental import pallas as pl
from jax.experimental.pallas import tpu as pltpu

F32 = jnp.float32
BF16 = jnp.bfloat16

D_MODEL = 1024
CHUNK = 64
NORM_EPS = 1e-6
DA_HEADS = 4
DA_QK_DIM = 64
DA_V_DIM = 128
DA_WIDTH = DA_HEADS * DA_V_DIM
ALIBI_SLOPES = tuple(2.0 ** (-8.0 * (h + 1) / DA_HEADS) for h in range(DA_HEADS))
RET_HEADS = 4
RET_QK_DIM = 64
RET_V_DIM = 128
RET_WIDTH = RET_HEADS * RET_V_DIM
RET_GAMMA = tuple(1.0 - 2.0 ** (-5.0 - h) for h in range(RET_HEADS))
DA_Q_COLS = DA_HEADS * 2 * DA_QK_DIM
DA_K_COLS = DA_HEADS * 2 * DA_QK_DIM
DA_V_COLS = DA_WIDTH
RET_Q_COLS = RET_HEADS * RET_QK_DIM
RET_K_COLS = RET_HEADS * RET_QK_DIM

V7X_LANES = 128
V7X_VMEM_BYTES = 64 * 1024 * 1024
VMEM_LIMIT_BYTES = V7X_VMEM_BYTES - 8 * 1024 * 1024

ROW_TILE = 512
ATT_TILE = 256
ATT_QTILE = 512
ATT_DEN_ROWS = 16
RET_BLOCK = 256
LOG2E = math.log2(math.e)
POS_TERMS = 3

_NT = (((1,), (1,)), ((), ()))
_TN = (((0,), (0,)), ((), ()))


def _rms(x):
    return x * lax.rsqrt(jnp.mean(x * x, axis=-1, keepdims=True) + NORM_EPS)


def _const_spec(shape):
    nd = len(shape)
    return pl.BlockSpec(shape, lambda *_: (0,) * nd, pipeline_mode=pl.Buffered(1))


def _inproj_kernel(x_ref, g_ref, wqT_ref, wk_ref, wvT_ref, wr_ref,
                   qT_ref, k_ref, vT_ref, rq_ref, rk_ref, rv_ref, rg_ref):
    h = (_rms(x_ref[...]) * g_ref[...]).astype(BF16)
    qT = lax.dot_general(wqT_ref[...], h, _NT, preferred_element_type=F32)
    qT_ref[...] = (qT * (DA_QK_DIM ** -0.5 * LOG2E)).astype(BF16)
    k = jnp.dot(h, wk_ref[...], preferred_element_type=F32)
    for hh in range(DA_HEADS):
        k_ref[hh] = k[:, hh * 128:(hh + 1) * 128].astype(BF16)
    vT = lax.dot_general(wvT_ref[...], h, _NT, preferred_element_type=F32)
    for j in range(ROW_TILE // ATT_TILE):
        vT_ref[j] = vT[:, j * ATT_TILE:(j + 1) * ATT_TILE].astype(BF16)
    r = jnp.dot(h, wr_ref[...], preferred_element_type=F32)
    rq_ref[...] = r[:, 0:256].astype(BF16)
    rk_ref[...] = r[:, 256:512].astype(BF16)
    rv_ref[...] = r[:, 512:1024].astype(BF16)
    rg_ref[...] = r[:, 1024:1536].astype(BF16)


def _inproj(x2, g, wqT, wk, wvT, wr):
    n = x2.shape[0]
    tm = ROW_TILE
    row = lambda i: (i, 0)
    out_shapes = (
        jax.ShapeDtypeStruct((DA_Q_COLS, n), BF16),
        jax.ShapeDtypeStruct((DA_HEADS, n, 128), BF16),
        jax.ShapeDtypeStruct((n // ATT_TILE, DA_V_COLS, ATT_TILE), BF16),
        jax.ShapeDtypeStruct((n, RET_Q_COLS), BF16),
        jax.ShapeDtypeStruct((n, RET_K_COLS), BF16),
        jax.ShapeDtypeStruct((n, RET_WIDTH), BF16),
        jax.ShapeDtypeStruct((n, RET_WIDTH), BF16),
    )
    out_specs = (
        pl.BlockSpec((DA_Q_COLS, tm), lambda i: (0, i)),
        pl.BlockSpec((DA_HEADS, tm, 128), lambda i: (0, i, 0)),
        pl.BlockSpec((tm // ATT_TILE, DA_V_COLS, ATT_TILE), lambda i: (i, 0, 0)),
        pl.BlockSpec((tm, RET_Q_COLS), row),
        pl.BlockSpec((tm, RET_K_COLS), row),
        pl.BlockSpec((tm, RET_WIDTH), row),
        pl.BlockSpec((tm, RET_WIDTH), row),
    )
    return pl.pallas_call(
        _inproj_kernel,
        grid=(n // tm,),
        in_specs=[
            pl.BlockSpec((tm, D_MODEL), row),
            _const_spec((1, D_MODEL)),
            _const_spec(wqT.shape),
            _const_spec(wk.shape),
            _const_spec(wvT.shape),
            _const_spec(wr.shape),
        ],
        out_specs=out_specs,
        out_shape=out_shapes,
        compiler_params=pltpu.CompilerParams(
            dimension_semantics=("arbitrary",), vmem_limit_bytes=VMEM_LIMIT_BYTES),
        name="inproj",
    )(x2, g, wqT, wk, wvT, wr)


def _attn_tables():
    t = ATT_TILE
    kk = np.arange(t)[:, None]
    qq = np.arange(t)[None, :]
    allowed = (kk // CHUNK) <= (qq // CHUNK)
    diag = np.empty((DA_HEADS, t, 2 * t), np.float32)
    prow = np.zeros((DA_HEADS, 128, 2 * ATT_QTILE), np.float32)
    for h, slope in enumerate(ALIBI_SLOPES):
        c = slope * LOG2E
        d = np.where(allowed, -2.0 * c * np.maximum(kk - qq, 0), -np.inf)
        diag[h] = np.concatenate([d, d], axis=1)
        rem = c
        for j in range(POS_TERMS):
            piece = float(np.asarray(rem, np.float32).astype(jnp.bfloat16).astype(np.float32))
            prow[h, j, :] = piece * CHUNK
            prow[h, POS_TERMS + j, :] = piece
            rem = rem - piece
    return diag, prow


def _pos_features(s):
    pos = np.arange(s)
    feat = np.zeros((s, 128), np.float32)
    for j in range(POS_TERMS):
        feat[:, j] = pos // CHUNK
        feat[:, POS_TERMS + j] = pos % CHUNK
    return feat


def _attn_kernel(lam_init, qT_ref, k_ref, pos_ref, vT_ref, diag_ref, prow_ref, lamv_ref, g_ref,
                 o_ref, acc_ref, sa_ref, sb_ref, sd_ref, pa_ref, pb_ref):
    t = ATT_TILE
    w = 2 * ATT_QTILE
    hw = w // 2
    qi = pl.program_id(2)
    nfull = 2 * qi
    qT = qT_ref[...]
    z = jnp.zeros((DA_QK_DIM, t), BF16)
    cols = []
    for half in range(2):
        qh = qT[:, half * t:(half + 1) * t]
        cols.append(jnp.concatenate([qh[:DA_QK_DIM], z], axis=0))
        cols.append(jnp.concatenate([z, qh[DA_QK_DIM:]], axis=0))
    qext = jnp.concatenate([jnp.concatenate(cols, axis=1), prow_ref[0]], axis=0)
    ones_rows = (lax.broadcasted_iota(jnp.int32, (ATT_DEN_ROWS, t), 0) == 0).astype(BF16)

    def keys(ki):
        rows = pl.ds(pl.multiple_of(ki * t, t), t)
        return jnp.concatenate([k_ref[rows, :], pos_ref[rows, :]], axis=1)

    def scores(ki):
        return jnp.dot(keys(ki), qext, preferred_element_type=F32)

    def values(ki, p):
        vt = jnp.concatenate([vT_ref[ki], ones_rows], axis=0)
        return jnp.dot(vt, p, preferred_element_type=F32)

    def softmax_tile(s, m):
        m_new = jnp.maximum(m, jnp.max(s, axis=0, keepdims=True))
        return m_new, jnp.exp2(m - m_new), jnp.exp2(s - m_new).astype(BF16)

    def half_step(ki, s_cur, s_next, p_prev, p_cur, m, alpha_prev):
        pv = values(jnp.maximum(ki - 1, 0), p_prev[...])
        s_next[...] = scores(ki + 1)
        m, alpha, p = softmax_tile(s_cur[...], m)
        acc_ref[...] = acc_ref[...] * alpha_prev + pv
        p_cur[...] = p
        return m, alpha

    def pair(j, carry):
        m, alpha = carry
        m, alpha = half_step(2 * j, sa_ref, sb_ref, pb_ref, pa_ref, m, alpha)
        m, alpha = half_step(2 * j + 1, sb_ref, sa_ref, pa_ref, pb_ref, m, alpha)
        return m, alpha

    acc_ref[...] = jnp.zeros_like(acc_ref)
    pb_ref[...] = jnp.zeros_like(pb_ref)
    sa_ref[...] = scores(0)
    sd_ref[...] = jnp.dot(keys(nfull + 1), qext[:, hw:], preferred_element_type=F32)
    m0 = jnp.full((1, w), -jnp.inf, F32)
    m, alpha_prev = lax.fori_loop(0, qi, pair, (m0, jnp.ones((1, w), F32)))

    diag = diag_ref[0]
    pv = values(jnp.maximum(nfull - 1, 0), pb_ref[...])
    s = sa_ref[...]
    s = jnp.concatenate([s[:, :hw] + diag, s[:, hw:]], axis=1)
    m, alpha, p = softmax_tile(s, m)
    acc = acc_ref[...] * alpha_prev + pv
    pv = values(nfull, p)
    mh, alpha_h, ph = softmax_tile(sd_ref[...] + diag, m[:, hw:])
    acc = acc * alpha + pv
    acc_h = acc[:, hw:] * alpha_h + values(nfull + 1, ph)

    lv = lamv_ref[...]
    lam = (jnp.exp(jnp.sum(lv[0:1] * lv[1:2], axis=-1, keepdims=True))
           - jnp.exp(jnp.sum(lv[2:3] * lv[3:4], axis=-1, keepdims=True)) + lam_init)

    def combine(a):
        num = a[:DA_V_DIM] * (1.0 / a[DA_V_DIM:DA_V_DIM + 1])
        return num[:, :t] - lam * num[:, t:]

    o = jnp.concatenate([combine(acc[:, :hw]), combine(acc_h)], axis=1)
    o = o * lax.rsqrt(jnp.mean(o * o, axis=0, keepdims=True) + NORM_EPS)
    o_ref[...] = (o.T * g_ref[...] * (1.0 - lam_init)).astype(o_ref.dtype)


def _attention(qT, k, vT, lamv, subln_g, batch, seq, lam_init):
    t = ATT_TILE
    tq = ATT_QTILE
    nq = seq // tq
    nk = seq // t
    n = batch * seq
    diag, prow = _attn_tables()
    diag = jnp.asarray(diag)
    prow = jnp.asarray(prow, BF16)
    pos = jnp.asarray(_pos_features(seq), BF16)
    return pl.pallas_call(
        functools.partial(_attn_kernel, lam_init),
        grid=(batch, DA_HEADS, nq),
        in_specs=[
            pl.BlockSpec((128, tq), lambda b, h, i: (h, b * nq + i)),
            pl.BlockSpec((None, seq, 128), lambda b, h, i: (h, b, 0)),
            pl.BlockSpec((seq, 128), lambda b, h, i: (0, 0)),
            pl.BlockSpec((nk, 128, t), lambda b, h, i: (b, h, 0)),
            pl.BlockSpec((1, t, 2 * t), lambda b, h, i: (h, 0, 0)),
            pl.BlockSpec((1, 128, 2 * tq), lambda b, h, i: (h, 0, 0)),
            pl.BlockSpec((4, DA_QK_DIM), lambda b, h, i: (0, 0)),
            pl.BlockSpec((1, DA_V_DIM), lambda b, h, i: (0, 0)),
        ],
        out_specs=pl.BlockSpec((tq, DA_V_DIM), lambda b, h, i: (b * nq + i, h)),
        out_shape=jax.ShapeDtypeStruct((n, DA_WIDTH), BF16),
        scratch_shapes=[
            pltpu.VMEM((DA_V_DIM + ATT_DEN_ROWS, 2 * tq), F32),
            pltpu.VMEM((t, 2 * tq), F32),
            pltpu.VMEM((t, 2 * tq), F32),
            pltpu.VMEM((t, tq), F32),
            pltpu.VMEM((t, 2 * tq), BF16),
            pltpu.VMEM((t, 2 * tq), BF16),
        ],
        compiler_params=pltpu.CompilerParams(
            dimension_semantics=("arbitrary", "arbitrary", "arbitrary"),
            vmem_limit_bytes=VMEM_LIMIT_BYTES),
        name="diff_attention",
    )(qT, k, pos, vT, diag, prow, lamv, subln_g)


def _ret_tables():
    L = RET_BLOCK
    i = np.arange(L)[:, None].astype(np.float64)
    j = np.arange(L)[None, :].astype(np.float64)
    allowed = (j // CHUNK) <= (i // CHUNK)
    dmat = np.empty((RET_HEADS, L, L), np.float32)
    qdec = np.zeros((RET_HEADS, L, 128), np.float32)
    kdec = np.zeros((RET_HEADS, L, 128), np.float32)
    cdec = np.empty((RET_HEADS, 1, 128), np.float32)
    lane = np.arange(128)[None, :]
    for h, gamma in enumerate(RET_GAMMA):
        mine = (lane // RET_QK_DIM) == (h % 2)
        dmat[h] = np.where(allowed, gamma ** np.abs(i - j), 0.0) * RET_QK_DIM ** -0.5
        qdec[h] = np.where(mine, gamma ** (i + 1.0), 0.0)
        kdec[h] = np.where(mine, gamma ** (L - 1.0 - i), 0.0) * RET_QK_DIM ** -0.5
        cdec[h] = gamma ** L
    return dmat, qdec, kdec, cdec


def _ret_kernel(q_ref, k_ref, v_ref, gate_ref, dmat_ref, qdec_ref, kdec_ref, cdec_ref, g_ref,
                o_ref, state_ref):
    @pl.when(pl.program_id(1) == 0)
    def _():
        state_ref[...] = jnp.zeros_like(state_ref)

    for h in range(RET_HEADS):
        pair = slice((h // 2) * 128, (h // 2 + 1) * 128)
        head = slice(h * RET_V_DIM, (h + 1) * RET_V_DIM)
        q = q_ref[:, pair].astype(F32)
        k = k_ref[:, pair]
        v = v_ref[:, head]
        qdec = qdec_ref[h]
        qm = jnp.where(qdec > 0.0, q, 0.0).astype(BF16)
        s = lax.dot_general(qm, k, _NT, preferred_element_type=F32) * dmat_ref[h]
        inner = jnp.dot(s.astype(BF16), v, preferred_element_type=F32)
        state = state_ref[h]
        cross = jnp.dot((q * qdec).astype(BF16), state.astype(BF16), preferred_element_type=F32)
        kd = (k.astype(F32) * kdec_ref[h]).astype(BF16)
        state_ref[h] = state * cdec_ref[h] + lax.dot_general(kd, v, _TN, preferred_element_type=F32)
        o = _rms(inner + cross) * g_ref[...]
        gate = gate_ref[:, head].astype(F32)
        o_ref[:, head] = (gate * (1.0 / (1.0 + jnp.exp(-gate))) * o).astype(o_ref.dtype)


def _retention(rq, rk, rv, rg, norm_g, batch, seq):
    L = RET_BLOCK
    nb = seq // L
    n = batch * seq
    dmat, qdec, kdec, cdec = (jnp.asarray(a) for a in _ret_tables())
    row = lambda b, i: (b * nb + i, 0)
    return pl.pallas_call(
        _ret_kernel,
        grid=(batch, nb),
        in_specs=[
            pl.BlockSpec((L, RET_Q_COLS), row),
            pl.BlockSpec((L, RET_K_COLS), row),
            pl.BlockSpec((L, RET_WIDTH), row),
            pl.BlockSpec((L, RET_WIDTH), row),
            _const_spec(dmat.shape),
            _const_spec(qdec.shape),
            _const_spec(kdec.shape),
            _const_spec(cdec.shape),
            _const_spec((1, RET_V_DIM)),
        ],
        out_specs=pl.BlockSpec((L, RET_WIDTH), row),
        out_shape=jax.ShapeDtypeStruct((n, RET_WIDTH), BF16),
        scratch_shapes=[pltpu.VMEM((RET_HEADS, 128, RET_V_DIM), F32)],
        compiler_params=pltpu.CompilerParams(
            dimension_semantics=("arbitrary", "arbitrary"),
            vmem_limit_bytes=VMEM_LIMIT_BYTES),
        name="retention",
    )(rq, rk, rv, rg, dmat, qdec, kdec, cdec, norm_g)


def _ffn_kernel(x_ref, oa_ref, or_ref, wo_ref, g1_ref, g2_ref, wg_ref, wu_ref, wd_ref, g3_ref,
                out_ref):
    mix = (jnp.dot(oa_ref[...], wo_ref[:DA_WIDTH, :], preferred_element_type=F32)
           + jnp.dot(or_ref[...], wo_ref[DA_WIDTH:, :], preferred_element_type=F32))
    x1 = x_ref[...] + _rms(mix) * g1_ref[...]
    h = (_rms(x1) * g2_ref[...]).astype(BF16)
    gate = jnp.dot(h, wg_ref[...], preferred_element_type=F32)
    up = jnp.dot(h, wu_ref[...], preferred_element_type=F32)
    f = (gate * (1.0 / (1.0 + jnp.exp(-gate))) * up).astype(BF16)
    y = jnp.dot(f, wd_ref[...], preferred_element_type=F32)
    out_ref[...] = x1 + _rms(y) * g3_ref[...]


def _ffn(x2, oa, orr, wo, g1, g2, wg, wu, wd, g3):
    n = x2.shape[0]
    tm = ROW_TILE
    row = lambda i: (i, 0)
    return pl.pallas_call(
        _ffn_kernel,
        grid=(n // tm,),
        in_specs=[
            pl.BlockSpec((tm, D_MODEL), row),
            pl.BlockSpec((tm, DA_WIDTH), row),
            pl.BlockSpec((tm, RET_WIDTH), row),
            _const_spec(wo.shape),
            _const_spec((1, D_MODEL)),
            _const_spec((1, D_MODEL)),
            _const_spec(wg.shape),
            _const_spec(wu.shape),
            _const_spec(wd.shape),
            _const_spec((1, D_MODEL)),
        ],
        out_specs=pl.BlockSpec((tm, D_MODEL), row),
        out_shape=jax.ShapeDtypeStruct((n, D_MODEL), F32),
        compiler_params=pltpu.CompilerParams(
            dimension_semantics=("arbitrary",), vmem_limit_bytes=VMEM_LIMIT_BYTES),
        name="outproj_ffn",
    )(x2, oa, orr, wo, g1, g2, wg, wu, wd, g3)


def kernel(x, pre_mix_g, w_in, lambda_q1, lambda_k1, lambda_q2, lambda_k2, da_subln_g, ret_norm_g,
           w_out, post_mix_g, pre_ffn_g, w_gate, w_up, w_down, post_ffn_g):
    batch, seq, d = x.shape
    depth = w_in.shape[0]
    assert d == D_MODEL and seq % ATT_QTILE == 0 and seq % RET_BLOCK == 0
    assert ATT_QTILE == 2 * ATT_TILE and ATT_TILE % CHUNK == 0 and RET_BLOCK % CHUNK == 0
    assert (batch * seq) % ROW_TILE == 0 and ROW_TILE % ATT_TILE == 0
    x2 = x.reshape(batch * seq, d)
    c1 = DA_Q_COLS
    c2 = c1 + DA_K_COLS
    c3 = c2 + DA_V_COLS
    for l in range(depth):
        w = w_in[l].astype(BF16)
        qT, k, vT, rq, rk, rv, rg = _inproj(
            x2, pre_mix_g[l][None, :], w[:, :c1].T, w[:, c1:c2], w[:, c2:c3].T, w[:, c3:])
        lam_init = 0.8 - 0.6 * math.exp(-0.3 * l)
        lamv = jnp.stack([lambda_q1[l], lambda_k1[l], lambda_q2[l], lambda_k2[l]]).astype(F32)
        o_da = _attention(qT, k, vT, lamv, da_subln_g[l][None, :].astype(F32), batch, seq, lam_init)
        o_ret = _retention(rq, rk, rv, rg, ret_norm_g[l][None, :].astype(F32), batch, seq)
        x2 = _ffn(x2, o_da, o_ret, w_out[l].astype(BF16), post_mix_g[l][None, :],
                  pre_ffn_g[l][None, :], w_gate[l].astype(BF16), w_up[l].astype(BF16),
                  w_down[l].astype(BF16), post_ffn_g[l][None, :])
    return x2.reshape(batch, seq, d)
```

```python
import functools
import math

import numpy as np
import jax
import jax.numpy as jnp
from jax import lax
from jax.experimental import pallas as pl
from jax.experimental.pallas import tpu as pltpu

F32 = jnp.float32
BF16 = jnp.bfloat16

D_MODEL = 1024
CHUNK = 64
NORM_EPS = 1e-6
DA_HEADS = 4
DA_QK_DIM = 64
DA_V_DIM = 128
DA_WIDTH = DA_HEADS * DA_V_DIM
ALIBI_SLOPES = tuple(2.0 ** (-8.0 * (h + 1) / DA_HEADS) for h in range(DA_HEADS))
RET_HEADS = 4
RET_QK_DIM = 64
RET_V_DIM = 128
RET_WIDTH = RET_HEADS * RET_V_DIM
RET_GAMMA = tuple(1.0 - 2.0 ** (-5.0 - h) for h in range(RET_HEADS))
DA_Q_COLS = DA_HEADS * 2 * DA_QK_DIM
DA_K_COLS = DA_HEADS * 2 * DA_QK_DIM
DA_V_COLS = DA_WIDTH
RET_Q_COLS = RET_HEADS * RET_QK_DIM
RET_K_COLS = RET_HEADS * RET_QK_DIM

V7X_LANES = 128
V7X_VMEM_BYTES = 64 * 1024 * 1024
VMEM_LIMIT_BYTES = V7X_VMEM_BYTES - 8 * 1024 * 1024

ROW_TILE = 512
ATT_TILE = 256
ATT_QTILE = 512
ATT_DEN_ROWS = 16
ATT_GROUP = 1
ATT_LOOKAHEAD = 4
RET_BLOCK = 256
LOG2E = math.log2(math.e)
POS_TERMS = 3

_NT = (((1,), (1,)), ((), ()))
_TN = (((0,), (0,)), ((), ()))


def _rms(x):
    return x * lax.rsqrt(jnp.mean(x * x, axis=-1, keepdims=True) + NORM_EPS)


def _const_spec(shape):
    nd = len(shape)
    return pl.BlockSpec(shape, lambda *_: (0,) * nd, pipeline_mode=pl.Buffered(1))


def _inproj_kernel(x_ref, g_ref, wqT_ref, wk_ref, wvT_ref, wr_ref,
                   qT_ref, k_ref, vT_ref, rq_ref, rk_ref, rv_ref, rg_ref):
    h = (_rms(x_ref[...]) * g_ref[...]).astype(BF16)
    qT = lax.dot_general(wqT_ref[...], h, _NT, preferred_element_type=F32)
    qT_ref[...] = (qT * (DA_QK_DIM ** -0.5 * LOG2E)).astype(BF16)
    k = jnp.dot(h, wk_ref[...], preferred_element_type=F32)
    for hh in range(DA_HEADS):
        k_ref[hh] = k[:, hh * 128:(hh + 1) * 128].astype(BF16)
    vT = lax.dot_general(wvT_ref[...], h, _NT, preferred_element_type=F32)
    for j in range(ROW_TILE // ATT_TILE):
        vT_ref[j] = vT[:, j * ATT_TILE:(j + 1) * ATT_TILE].astype(BF16)
    r = jnp.dot(h, wr_ref[...], preferred_element_type=F32)
    rq_ref[...] = r[:, 0:256].astype(BF16)
    rk_ref[...] = r[:, 256:512].astype(BF16)
    rv_ref[...] = r[:, 512:1024].astype(BF16)
    rg_ref[...] = r[:, 1024:1536].astype(BF16)


def _inproj(x2, g, wqT, wk, wvT, wr):
    n = x2.shape[0]
    tm = ROW_TILE
    row = lambda i: (i, 0)
    out_shapes = (
        jax.ShapeDtypeStruct((DA_Q_COLS, n), BF16),
        jax.ShapeDtypeStruct((DA_HEADS, n, 128), BF16),
        jax.ShapeDtypeStruct((n // ATT_TILE, DA_V_COLS, ATT_TILE), BF16),
        jax.ShapeDtypeStruct((n, RET_Q_COLS), BF16),
        jax.ShapeDtypeStruct((n, RET_K_COLS), BF16),
        jax.ShapeDtypeStruct((n, RET_WIDTH), BF16),
        jax.ShapeDtypeStruct((n, RET_WIDTH), BF16),
    )
    out_specs = (
        pl.BlockSpec((DA_Q_COLS, tm), lambda i: (0, i)),
        pl.BlockSpec((DA_HEADS, tm, 128), lambda i: (0, i, 0)),
        pl.BlockSpec((tm // ATT_TILE, DA_V_COLS, ATT_TILE), lambda i: (i, 0, 0)),
        pl.BlockSpec((tm, RET_Q_COLS), row),
        pl.BlockSpec((tm, RET_K_COLS), row),
        pl.BlockSpec((tm, RET_WIDTH), row),
        pl.BlockSpec((tm, RET_WIDTH), row),
    )
    return pl.pallas_call(
        _inproj_kernel,
        grid=(n // tm,),
        in_specs=[
            pl.BlockSpec((tm, D_MODEL), row),
            _const_spec((1, D_MODEL)),
            _const_spec(wqT.shape),
            _const_spec(wk.shape),
            _const_spec(wvT.shape),
            _const_spec(wr.shape),
        ],
        out_specs=out_specs,
        out_shape=out_shapes,
        compiler_params=pltpu.CompilerParams(
            dimension_semantics=("arbitrary",), vmem_limit_bytes=VMEM_LIMIT_BYTES),
        name="inproj",
    )(x2, g, wqT, wk, wvT, wr)


def _attn_tables():
    t = ATT_TILE
    kk = np.arange(t)[:, None]
    qq = np.arange(t)[None, :]
    allowed = (kk // CHUNK) <= (qq // CHUNK)
    diag = np.empty((DA_HEADS, t, 2 * t), np.float32)
    prow = np.zeros((DA_HEADS, 128, 2 * ATT_QTILE), np.float32)
    for h, slope in enumerate(ALIBI_SLOPES):
        c = slope * LOG2E
        d = np.where(allowed, -2.0 * c * np.maximum(kk - qq, 0), -np.inf)
        diag[h] = np.concatenate([d, d], axis=1)
        rem = c
        for j in range(POS_TERMS):
            piece = float(np.asarray(rem, np.float32).astype(jnp.bfloat16).astype(np.float32))
            prow[h, j, :] = piece * CHUNK
            prow[h, POS_TERMS + j, :] = piece
            rem = rem - piece
    return diag, prow


def _pos_features(s):
    pos = np.arange(s)
    feat = np.zeros((s, 128), np.float32)
    for j in range(POS_TERMS):
        feat[:, j] = pos // CHUNK
        feat[:, POS_TERMS + j] = pos % CHUNK
    return feat


def _attn_kernel(lam_init, qT_ref, k_ref, pos_ref, vT_ref, diag_ref, prow_ref, lamv_ref, g_ref,
                 o_ref, acc_ref):
    t = ATT_TILE
    tq = ATT_QTILE
    w = 2 * tq
    hw = w // 2
    nq = qT_ref.shape[1] // tq
    z = jnp.zeros((DA_QK_DIM, t), BF16)
    ones_rows = (lax.broadcasted_iota(jnp.int32, (ATT_DEN_ROWS, t), 0) == 0).astype(BF16)
    diag = diag_ref[0]
    lv = lamv_ref[...]
    lam = (jnp.exp(jnp.sum(lv[0:1] * lv[1:2], axis=-1, keepdims=True))
           - jnp.exp(jnp.sum(lv[2:3] * lv[3:4], axis=-1, keepdims=True)) + lam_init)

    def keys(k0, nk):
        rows = slice(k0 * t, (k0 + nk) * t)
        return jnp.concatenate([k_ref[rows, :], pos_ref[rows, :]], axis=1)

    def values_t(k0, nk):
        return jnp.concatenate(
            [jnp.concatenate([vT_ref[ki], ones_rows], axis=0) for ki in range(k0, k0 + nk)], axis=1)

    def colmax(a):
        return jnp.max(a, axis=0, keepdims=True)

    def prob(a, m):
        return jnp.exp2(a - m).astype(BF16)

    def combine(a):
        num = a[:DA_V_DIM] * (1.0 / a[DA_V_DIM:DA_V_DIM + 1])
        return num[:, :t] - lam * num[:, t:]

    def query_operand(qi):
        cols = []
        for half in range(2):
            qh = qT_ref[:, qi * tq + half * t:qi * tq + (half + 1) * t]
            cols.append(jnp.concatenate([qh[:DA_QK_DIM], z], axis=0))
            cols.append(jnp.concatenate([z, qh[DA_QK_DIM:]], axis=0))
        return jnp.concatenate([jnp.concatenate(cols, axis=1), prow_ref[0]], axis=0)

    tasks = []
    for qi in range(nq):
        tasks.append((qi, "first", 2 * qi))
        tasks += [(qi, "full", k0) for k0 in range(0, 2 * qi, ATT_GROUP)]
        tasks.append((qi, "last", 2 * qi + 1))

    qexts = {}

    def scores(task):
        qi, kind, k0 = task
        if qi not in qexts:
            qexts[qi] = query_operand(qi)
        qext = qexts[qi]
        if kind == "last":
            return jnp.dot(keys(k0, 1), qext[:, hw:], preferred_element_type=F32)
        return jnp.dot(keys(k0, ATT_GROUP if kind == "full" else 1), qext,
                       preferred_element_type=F32)

    pending = [scores(task) for task in tasks[:ATT_LOOKAHEAD]]
    m = None
    for idx, task in enumerate(tasks):
        qi, kind, k0 = task
        s = pending.pop(0)
        if idx + ATT_LOOKAHEAD < len(tasks):
            pending.append(scores(tasks[idx + ATT_LOOKAHEAD]))
        acc = acc_ref.at[qi % 2]
        if kind == "first":
            s = jnp.concatenate([s[:, :hw] + diag, s[:, hw:]], axis=1)
            m = colmax(s)
            acc[...] = jnp.dot(values_t(k0, 1), prob(s, m), preferred_element_type=F32)
        elif kind == "full":
            m_new = jnp.maximum(m, colmax(s))
            pv = jnp.dot(values_t(k0, ATT_GROUP), prob(s, m_new), preferred_element_type=F32)
            acc[...] = acc[...] * jnp.exp2(m - m_new) + pv
            m = m_new
        else:
            s = s + diag
            mh = m[:, hw:]
            mh_new = jnp.maximum(mh, colmax(s))
            pv = jnp.dot(values_t(k0, 1), prob(s, mh_new), preferred_element_type=F32)
            acc_h = acc[:, hw:] * jnp.exp2(mh - mh_new) + pv
            o = jnp.concatenate([combine(acc[:, :hw]), combine(acc_h)], axis=1)
            o = o * lax.rsqrt(jnp.mean(o * o, axis=0, keepdims=True) + NORM_EPS)
            o_ref[qi * tq:(qi + 1) * tq, :] = (
                o.T * g_ref[...] * (1.0 - lam_init)).astype(o_ref.dtype)


def _attention(qT, k, vT, lamv, subln_g, batch, seq, lam_init):
    t = ATT_TILE
    tq = ATT_QTILE
    nq = seq // tq
    nk = seq // t
    n = batch * seq
    diag, prow = _attn_tables()
    diag = jnp.asarray(diag)
    prow = jnp.asarray(prow, BF16)
    pos = jnp.asarray(_pos_features(seq), BF16)
    return pl.pallas_call(
        functools.partial(_attn_kernel, lam_init),
        grid=(batch, DA_HEADS),
        in_specs=[
            pl.BlockSpec((128, seq), lambda b, h: (h, b)),
            pl.BlockSpec((None, seq, 128), lambda b, h: (h, b, 0)),
            pl.BlockSpec((seq, 128), lambda b, h: (0, 0)),
            pl.BlockSpec((nk, 128, t), lambda b, h: (b, h, 0)),
            pl.BlockSpec((1, t, 2 * t), lambda b, h: (h, 0, 0)),
            pl.BlockSpec((1, 128, 2 * tq), lambda b, h: (h, 0, 0)),
            pl.BlockSpec((4, DA_QK_DIM), lambda b, h: (0, 0)),
            pl.BlockSpec((1, DA_V_DIM), lambda b, h: (0, 0)),
        ],
        out_specs=pl.BlockSpec((seq, DA_V_DIM), lambda b, h: (b, h)),
        out_shape=jax.ShapeDtypeStruct((n, DA_WIDTH), BF16),
        scratch_shapes=[pltpu.VMEM((2, DA_V_DIM + ATT_DEN_ROWS, 2 * tq), F32)],
        compiler_params=pltpu.CompilerParams(
            dimension_semantics=("arbitrary", "arbitrary"),
            vmem_limit_bytes=VMEM_LIMIT_BYTES),
        name="diff_attention",
    )(qT, k, pos, vT, diag, prow, lamv, subln_g)


def _ret_tables():
    L = RET_BLOCK
    i = np.arange(L)[:, None].astype(np.float64)
    j = np.arange(L)[None, :].astype(np.float64)
    allowed = (j // CHUNK) <= (i // CHUNK)
    dmat = np.empty((RET_HEADS, L, L), np.float32)
    qdec = np.zeros((RET_HEADS, L, 128), np.float32)
    kdec = np.zeros((RET_HEADS, L, 128), np.float32)
    cdec = np.empty((RET_HEADS, 1, 128), np.float32)
    lane = np.arange(128)[None, :]
    for h, gamma in enumerate(RET_GAMMA):
        mine = (lane // RET_QK_DIM) == (h % 2)
        dmat[h] = np.where(allowed, gamma ** np.abs(i - j), 0.0) * RET_QK_DIM ** -0.5
        qdec[h] = np.where(mine, gamma ** (i + 1.0), 0.0)
        kdec[h] = np.where(mine, gamma ** (L - 1.0 - i), 0.0) * RET_QK_DIM ** -0.5
        cdec[h] = gamma ** L
    return dmat, qdec, kdec, cdec


def _ret_kernel(q_ref, k_ref, v_ref, gate_ref, dmat_ref, qdec_ref, kdec_ref, cdec_ref, g_ref,
                o_ref, state_ref):
    @pl.when(pl.program_id(1) == 0)
    def _():
        state_ref[...] = jnp.zeros_like(state_ref)

    for h in range(RET_HEADS):
        pair = slice((h // 2) * 128, (h // 2 + 1) * 128)
        head = slice(h * RET_V_DIM, (h + 1) * RET_V_DIM)
        q = q_ref[:, pair].astype(F32)
        k = k_ref[:, pair]
        v = v_ref[:, head]
        qdec = qdec_ref[h]
        qm = jnp.where(qdec > 0.0, q, 0.0).astype(BF16)
        s = lax.dot_general(qm, k, _NT, preferred_element_type=F32) * dmat_ref[h]
        inner = jnp.dot(s.astype(BF16), v, preferred_element_type=F32)
        state = state_ref[h]
        cross = jnp.dot((q * qdec).astype(BF16), state.astype(BF16), preferred_element_type=F32)
        kd = (k.astype(F32) * kdec_ref[h]).astype(BF16)
        state_ref[h] = state * cdec_ref[h] + lax.dot_general(kd, v, _TN, preferred_element_type=F32)
        o = _rms(inner + cross) * g_ref[...]
        gate = gate_ref[:, head].astype(F32)
        o_ref[:, head] = (gate * (1.0 / (1.0 + jnp.exp(-gate))) * o).astype(o_ref.dtype)


def _retention(rq, rk, rv, rg, norm_g, batch, seq):
    L = RET_BLOCK
    nb = seq // L
    n = batch * seq
    dmat, qdec, kdec, cdec = (jnp.asarray(a) for a in _ret_tables())
    row = lambda b, i: (b * nb + i, 0)
    return pl.pallas_call(
        _ret_kernel,
        grid=(batch, nb),
        in_specs=[
            pl.BlockSpec((L, RET_Q_COLS), row),
            pl.BlockSpec((L, RET_K_COLS), row),
            pl.BlockSpec((L, RET_WIDTH), row),
            pl.BlockSpec((L, RET_WIDTH), row),
            _const_spec(dmat.shape),
            _const_spec(qdec.shape),
            _const_spec(kdec.shape),
            _const_spec(cdec.shape),
            _const_spec((1, RET_V_DIM)),
        ],
        out_specs=pl.BlockSpec((L, RET_WIDTH), row),
        out_shape=jax.ShapeDtypeStruct((n, RET_WIDTH), BF16),
        scratch_shapes=[pltpu.VMEM((RET_HEADS, 128, RET_V_DIM), F32)],
        compiler_params=pltpu.CompilerParams(
            dimension_semantics=("arbitrary", "arbitrary"),
            vmem_limit_bytes=VMEM_LIMIT_BYTES),
        name="retention",
    )(rq, rk, rv, rg, dmat, qdec, kdec, cdec, norm_g)


def _ffn_kernel(x_ref, oa_ref, or_ref, wo_ref, g1_ref, g2_ref, wg_ref, wu_ref, wd_ref, g3_ref,
                out_ref):
    mix = (jnp.dot(oa_ref[...], wo_ref[:DA_WIDTH, :], preferred_element_type=F32)
           + jnp.dot(or_ref[...], wo_ref[DA_WIDTH:, :], preferred_element_type=F32))
    x1 = x_ref[...] + _rms(mix) * g1_ref[...]
    h = (_rms(x1) * g2_ref[...]).astype(BF16)
    gate = jnp.dot(h, wg_ref[...], preferred_element_type=F32)
    up = jnp.dot(h, wu_ref[...], preferred_element_type=F32)
    f = (gate * (1.0 / (1.0 + jnp.exp(-gate))) * up).astype(BF16)
    y = jnp.dot(f, wd_ref[...], preferred_element_type=F32)
    out_ref[...] = x1 + _rms(y) * g3_ref[...]


def _ffn(x2, oa, orr, wo, g1, g2, wg, wu, wd, g3):
    n = x2.shape[0]
    tm = ROW_TILE
    row = lambda i: (i, 0)
    return pl.pallas_call(
        _ffn_kernel,
        grid=(n // tm,),
        in_specs=[
            pl.BlockSpec((tm, D_MODEL), row),
            pl.BlockSpec((tm, DA_WIDTH), row),
            pl.BlockSpec((tm, RET_WIDTH), row),
            _const_spec(wo.shape),
            _const_spec((1, D_MODEL)),
            _const_spec((1, D_MODEL)),
            _const_spec(wg.shape),
            _const_spec(wu.shape),
            _const_spec(wd.shape),
            _const_spec((1, D_MODEL)),
        ],
        out_specs=pl.BlockSpec((tm, D_MODEL), row),
        out_shape=jax.ShapeDtypeStruct((n, D_MODEL), F32),
        compiler_params=pltpu.CompilerParams(
            dimension_semantics=("arbitrary",), vmem_limit_bytes=VMEM_LIMIT_BYTES),
        name="outproj_ffn",
    )(x2, oa, orr, wo, g1, g2, wg, wu, wd, g3)


def kernel(x, pre_mix_g, w_in, lambda_q1, lambda_k1, lambda_q2, lambda_k2, da_subln_g, ret_norm_g,
           w_out, post_mix_g, pre_ffn_g, w_gate, w_up, w_down, post_ffn_g):
    batch, seq, d = x.shape
    depth = w_in.shape[0]
    assert d == D_MODEL and seq % ATT_QTILE == 0 and seq % RET_BLOCK == 0
    assert ATT_QTILE == 2 * ATT_TILE and ATT_TILE % CHUNK == 0 and RET_BLOCK % CHUNK == 0
    assert (batch * seq) % ROW_TILE == 0 and ROW_TILE % ATT_TILE == 0
    x2 = x.reshape(batch * seq, d)
    c1 = DA_Q_COLS
    c2 = c1 + DA_K_COLS
    c3 = c2 + DA_V_COLS
    for l in range(depth):
        w = w_in[l].astype(BF16)
        qT, k, vT, rq, rk, rv, rg = _inproj(
            x2, pre_mix_g[l][None, :], w[:, :c1].T, w[:, c1:c2], w[:, c2:c3].T, w[:, c3:])
        lam_init = 0.8 - 0.6 * math.exp(-0.3 * l)
        lamv = jnp.stack([lambda_q1[l], lambda_k1[l], lambda_q2[l], lambda_k2[l]]).astype(F32)
        o_da = _attention(qT, k, vT, lamv, da_subln_g[l][None, :].astype(F32), batch, seq, lam_init)
        o_ret = _retention(rq, rk, rv, rg, ret_norm_g[l][None, :].astype(F32), batch, seq)
        x2 = _ffn(x2, o_da, o_ret, w_out[l].astype(BF16), post_mix_g[l][None, :],
                  pre_ffn_g[l][None, :], w_gate[l].astype(BF16), w_up[l].astype(BF16),
                  w_down[l].astype(BF16), post_ffn_g[l][None, :])
    return x2.reshape(batch, seq, d)
```

```python
import functools
import math

import numpy as np
import jax
import jax.numpy as jnp
from jax import lax
from jax.experimental import pallas as pl
from jax.experimental.pallas import tpu as pltpu

F32 = jnp.float32
BF16 = jnp.bfloat16

D_MODEL = 1024
CHUNK = 64
NORM_EPS = 1e-6
DA_HEADS = 4
DA_QK_DIM = 64
DA_V_DIM = 128
DA_WIDTH = DA_HEADS * DA_V_DIM
ALIBI_SLOPES = tuple(2.0 ** (-8.0 * (h + 1) / DA_HEADS) for h in range(DA_HEADS))
RET_HEADS = 4
RET_QK_DIM = 64
RET_V_DIM = 128
RET_WIDTH = RET_HEADS * RET_V_DIM
RET_GAMMA = tuple(1.0 - 2.0 ** (-5.0 - h) for h in range(RET_HEADS))
DA_Q_COLS = DA_HEADS * 2 * DA_QK_DIM
DA_K_COLS = DA_HEADS * 2 * DA_QK_DIM
DA_V_COLS = DA_WIDTH
RET_Q_COLS = RET_HEADS * RET_QK_DIM
RET_K_COLS = RET_HEADS * RET_QK_DIM

V7X_LANES = 128
V7X_VMEM_BYTES = 64 * 1024 * 1024
VMEM_LIMIT_BYTES = V7X_VMEM_BYTES - 8 * 1024 * 1024

ROW_TILE = 512
ATT_TILE = 256
ATT_QTILE = 512
ATT_DEN_ROWS = 16
ATT_GROUP = 1
ATT_LOOKAHEAD = 4
RET_BLOCK = 256
RET_LOOKAHEAD = 2
LOG2E = math.log2(math.e)
POS_TERMS = 3

_NT = (((1,), (1,)), ((), ()))
_TN = (((0,), (0,)), ((), ()))


def _rms(x):
    return x * lax.rsqrt(jnp.mean(x * x, axis=-1, keepdims=True) + NORM_EPS)


def _const_spec(shape):
    nd = len(shape)
    return pl.BlockSpec(shape, lambda *_: (0,) * nd, pipeline_mode=pl.Buffered(1))


def _inproj_kernel(x_ref, g_ref, wqT_ref, wk_ref, wvT_ref, wr_ref,
                   qT_ref, k_ref, vT_ref, rq_ref, rk_ref, rv_ref, rg_ref):
    h = (_rms(x_ref[...]) * g_ref[...]).astype(BF16)
    qT = lax.dot_general(wqT_ref[...], h, _NT, preferred_element_type=F32)
    qT_ref[...] = (qT * (DA_QK_DIM ** -0.5 * LOG2E)).astype(BF16)
    k = jnp.dot(h, wk_ref[...], preferred_element_type=F32)
    for hh in range(DA_HEADS):
        k_ref[hh] = k[:, hh * 128:(hh + 1) * 128].astype(BF16)
    vT = lax.dot_general(wvT_ref[...], h, _NT, preferred_element_type=F32)
    for j in range(ROW_TILE // ATT_TILE):
        vT_ref[j] = vT[:, j * ATT_TILE:(j + 1) * ATT_TILE].astype(BF16)
    r = jnp.dot(h, wr_ref[...], preferred_element_type=F32)
    rq_ref[...] = r[:, 0:256].astype(BF16)
    rk_ref[...] = r[:, 256:512].astype(BF16)
    rv_ref[...] = r[:, 512:1024].astype(BF16)
    rg_ref[...] = r[:, 1024:1536].astype(BF16)


def _inproj(x2, g, wqT, wk, wvT, wr):
    n = x2.shape[0]
    tm = ROW_TILE
    row = lambda i: (i, 0)
    out_shapes = (
        jax.ShapeDtypeStruct((DA_Q_COLS, n), BF16),
        jax.ShapeDtypeStruct((DA_HEADS, n, 128), BF16),
        jax.ShapeDtypeStruct((n // ATT_TILE, DA_V_COLS, ATT_TILE), BF16),
        jax.ShapeDtypeStruct((n, RET_Q_COLS), BF16),
        jax.ShapeDtypeStruct((n, RET_K_COLS), BF16),
        jax.ShapeDtypeStruct((n, RET_WIDTH), BF16),
        jax.ShapeDtypeStruct((n, RET_WIDTH), BF16),
    )
    out_specs = (
        pl.BlockSpec((DA_Q_COLS, tm), lambda i: (0, i)),
        pl.BlockSpec((DA_HEADS, tm, 128), lambda i: (0, i, 0)),
        pl.BlockSpec((tm // ATT_TILE, DA_V_COLS, ATT_TILE), lambda i: (i, 0, 0)),
        pl.BlockSpec((tm, RET_Q_COLS), row),
        pl.BlockSpec((tm, RET_K_COLS), row),
        pl.BlockSpec((tm, RET_WIDTH), row),
        pl.BlockSpec((tm, RET_WIDTH), row),
    )
    return pl.pallas_call(
        _inproj_kernel,
        grid=(n // tm,),
        in_specs=[
            pl.BlockSpec((tm, D_MODEL), row),
            _const_spec((1, D_MODEL)),
            _const_spec(wqT.shape),
            _const_spec(wk.shape),
            _const_spec(wvT.shape),
            _const_spec(wr.shape),
        ],
        out_specs=out_specs,
        out_shape=out_shapes,
        compiler_params=pltpu.CompilerParams(
            dimension_semantics=("arbitrary",), vmem_limit_bytes=VMEM_LIMIT_BYTES),
        name="inproj",
    )(x2, g, wqT, wk, wvT, wr)


def _attn_tables():
    t = ATT_TILE
    kk = np.arange(t)[:, None]
    qq = np.arange(t)[None, :]
    allowed = (kk // CHUNK) <= (qq // CHUNK)
    diag = np.empty((DA_HEADS, t, 2 * t), np.float32)
    prow = np.zeros((DA_HEADS, 128, 2 * ATT_QTILE), np.float32)
    for h, slope in enumerate(ALIBI_SLOPES):
        c = slope * LOG2E
        d = np.where(allowed, -2.0 * c * np.maximum(kk - qq, 0), -np.inf)
        diag[h] = np.concatenate([d, d], axis=1)
        rem = c
        for j in range(POS_TERMS):
            piece = float(np.asarray(rem, np.float32).astype(jnp.bfloat16).astype(np.float32))
            prow[h, j, :] = piece * CHUNK
            prow[h, POS_TERMS + j, :] = piece
            rem = rem - piece
    return diag, prow


def _pos_features(s):
    pos = np.arange(s)
    feat = np.zeros((s, 128), np.float32)
    for j in range(POS_TERMS):
        feat[:, j] = pos // CHUNK
        feat[:, POS_TERMS + j] = pos % CHUNK
    return feat


def _attn_kernel(lam_init, qT_ref, k_ref, pos_ref, vT_ref, diag_ref, prow_ref, lamv_ref, g_ref,
                 o_ref, acc_ref):
    t = ATT_TILE
    tq = ATT_QTILE
    w = 2 * tq
    hw = w // 2
    nq = qT_ref.shape[1] // tq
    z = jnp.zeros((DA_QK_DIM, t), BF16)
    ones_rows = (lax.broadcasted_iota(jnp.int32, (ATT_DEN_ROWS, t), 0) == 0).astype(BF16)
    diag = diag_ref[0]
    lv = lamv_ref[...]
    lam = (jnp.exp(jnp.sum(lv[0:1] * lv[1:2], axis=-1, keepdims=True))
           - jnp.exp(jnp.sum(lv[2:3] * lv[3:4], axis=-1, keepdims=True)) + lam_init)

    def keys(k0, nk):
        rows = slice(k0 * t, (k0 + nk) * t)
        return jnp.concatenate([k_ref[rows, :], pos_ref[rows, :]], axis=1)

    def values_t(k0, nk):
        return jnp.concatenate(
            [jnp.concatenate([vT_ref[ki], ones_rows], axis=0) for ki in range(k0, k0 + nk)], axis=1)

    def colmax(a):
        return jnp.max(a, axis=0, keepdims=True)

    def prob(a, m):
        return jnp.exp2(a - m).astype(BF16)

    def combine(a):
        num = a[:DA_V_DIM] * (1.0 / a[DA_V_DIM:DA_V_DIM + 1])
        return num[:, :t] - lam * num[:, t:]

    def query_operand(qi):
        cols = []
        for half in range(2):
            qh = qT_ref[:, qi * tq + half * t:qi * tq + (half + 1) * t]
            cols.append(jnp.concatenate([qh[:DA_QK_DIM], z], axis=0))
            cols.append(jnp.concatenate([z, qh[DA_QK_DIM:]], axis=0))
        return jnp.concatenate([jnp.concatenate(cols, axis=1), prow_ref[0]], axis=0)

    tasks = []
    for qi in range(nq):
        tasks.append((qi, "first", 2 * qi))
        tasks += [(qi, "full", k0) for k0 in range(0, 2 * qi, ATT_GROUP)]
        tasks.append((qi, "last", 2 * qi + 1))

    qexts = {}

    def scores(task):
        qi, kind, k0 = task
        if qi not in qexts:
            qexts[qi] = query_operand(qi)
        qext = qexts[qi]
        if kind == "last":
            return jnp.dot(keys(k0, 1), qext[:, hw:], preferred_element_type=F32)
        return jnp.dot(keys(k0, ATT_GROUP if kind == "full" else 1), qext,
                       preferred_element_type=F32)

    pending = [scores(task) for task in tasks[:ATT_LOOKAHEAD]]
    m = None
    for idx, task in enumerate(tasks):
        qi, kind, k0 = task
        s = pending.pop(0)
        if idx + ATT_LOOKAHEAD < len(tasks):
            pending.append(scores(tasks[idx + ATT_LOOKAHEAD]))
        acc = acc_ref.at[qi % 2]
        if kind == "first":
            s = jnp.concatenate([s[:, :hw] + diag, s[:, hw:]], axis=1)
            m = colmax(s)
            acc[...] = jnp.dot(values_t(k0, 1), prob(s, m), preferred_element_type=F32)
        elif kind == "full":
            m_new = jnp.maximum(m, colmax(s))
            pv = jnp.dot(values_t(k0, ATT_GROUP), prob(s, m_new), preferred_element_type=F32)
            acc[...] = acc[...] * jnp.exp2(m - m_new) + pv
            m = m_new
        else:
            s = s + diag
            mh = m[:, hw:]
            mh_new = jnp.maximum(mh, colmax(s))
            pv = jnp.dot(values_t(k0, 1), prob(s, mh_new), preferred_element_type=F32)
            acc_h = acc[:, hw:] * jnp.exp2(mh - mh_new) + pv
            o = jnp.concatenate([combine(acc[:, :hw]), combine(acc_h)], axis=1)
            o = o * lax.rsqrt(jnp.mean(o * o, axis=0, keepdims=True) + NORM_EPS)
            o_ref[qi * tq:(qi + 1) * tq, :] = (
                o.T * g_ref[...] * (1.0 - lam_init)).astype(o_ref.dtype)


def _attention(qT, k, vT, lamv, subln_g, batch, seq, lam_init):
    t = ATT_TILE
    tq = ATT_QTILE
    nq = seq // tq
    nk = seq // t
    n = batch * seq
    diag, prow = _attn_tables()
    diag = jnp.asarray(diag)
    prow = jnp.asarray(prow, BF16)
    pos = jnp.asarray(_pos_features(seq), BF16)
    return pl.pallas_call(
        functools.partial(_attn_kernel, lam_init),
        grid=(batch, DA_HEADS),
        in_specs=[
            pl.BlockSpec((128, seq), lambda b, h: (h, b)),
            pl.BlockSpec((None, seq, 128), lambda b, h: (h, b, 0)),
            pl.BlockSpec((seq, 128), lambda b, h: (0, 0)),
            pl.BlockSpec((nk, 128, t), lambda b, h: (b, h, 0)),
            pl.BlockSpec((1, t, 2 * t), lambda b, h: (h, 0, 0)),
            pl.BlockSpec((1, 128, 2 * tq), lambda b, h: (h, 0, 0)),
            pl.BlockSpec((4, DA_QK_DIM), lambda b, h: (0, 0)),
            pl.BlockSpec((1, DA_V_DIM), lambda b, h: (0, 0)),
        ],
        out_specs=pl.BlockSpec((seq, DA_V_DIM), lambda b, h: (b, h)),
        out_shape=jax.ShapeDtypeStruct((n, DA_WIDTH), BF16),
        scratch_shapes=[pltpu.VMEM((2, DA_V_DIM + ATT_DEN_ROWS, 2 * tq), F32)],
        compiler_params=pltpu.CompilerParams(
            dimension_semantics=("arbitrary", "arbitrary"),
            vmem_limit_bytes=VMEM_LIMIT_BYTES),
        name="diff_attention",
    )(qT, k, pos, vT, diag, prow, lamv, subln_g)


def _ret_tables():
    L = RET_BLOCK
    i = np.arange(L)[:, None].astype(np.float64)
    j = np.arange(L)[None, :].astype(np.float64)
    allowed = (j // CHUNK) <= (i // CHUNK)
    dmat = np.empty((RET_HEADS, L, L), np.float32)
    qdec = np.zeros((RET_HEADS, L, 128), np.float32)
    kdec = np.zeros((RET_HEADS, L, 128), np.float32)
    cdec = np.empty((RET_HEADS, 1, 128), np.float32)
    lane = np.arange(128)[None, :]
    for h, gamma in enumerate(RET_GAMMA):
        mine = (lane // RET_QK_DIM) == (h % 2)
        dmat[h] = np.where(allowed, gamma ** np.abs(i - j), 0.0) * RET_QK_DIM ** -0.5
        qdec[h] = np.where(mine, gamma ** (i + 1.0), 0.0)
        kdec[h] = np.where(mine, gamma ** (L - 1.0 - i), 0.0) * RET_QK_DIM ** -0.5
        cdec[h] = gamma ** L
    return dmat, qdec, kdec, cdec


def _ret_kernel(q_ref, k_ref, v_ref, gate_ref, dmat_ref, qdec_ref, kdec_ref, cdec_ref, g_ref,
                o_ref):
    L = RET_BLOCK
    tasks = [(blk, h) for blk in range(q_ref.shape[0] // L) for h in range(RET_HEADS)]

    def operands(task):
        blk, h = task
        rows = slice(blk * L, (blk + 1) * L)
        pair = slice((h // 2) * 128, (h // 2 + 1) * 128)
        return rows, slice(h * RET_V_DIM, (h + 1) * RET_V_DIM), q_ref[rows, pair], k_ref[rows, pair]

    def scores(task):
        _, h = task
        _, _, q, k = operands(task)
        qm = jnp.where(qdec_ref[h] > 0.0, q, jnp.zeros_like(q))
        s = lax.dot_general(qm, k, _NT, preferred_element_type=F32) * dmat_ref[h]
        return s.astype(BF16)

    states = [jnp.zeros((128, RET_V_DIM), F32) for _ in range(RET_HEADS)]
    pending = [scores(task) for task in tasks[:RET_LOOKAHEAD]]
    for idx, task in enumerate(tasks):
        _, h = task
        s = pending.pop(0)
        if idx + RET_LOOKAHEAD < len(tasks):
            pending.append(scores(tasks[idx + RET_LOOKAHEAD]))
        rows, head, q, k = operands(task)
        v = v_ref[rows, head]
        inner = jnp.dot(s, v, preferred_element_type=F32)
        qd = (q.astype(F32) * qdec_ref[h]).astype(BF16)
        cross = jnp.dot(qd, states[h].astype(BF16), preferred_element_type=F32)
        kd = (k.astype(F32) * kdec_ref[h]).astype(BF16)
        states[h] = states[h] * cdec_ref[h] + lax.dot_general(
            kd, v, _TN, preferred_element_type=F32)
        o = _rms(inner + cross) * g_ref[...]
        gate = gate_ref[rows, head].astype(F32)
        o_ref[rows, head] = (gate * (1.0 / (1.0 + jnp.exp(-gate))) * o).astype(o_ref.dtype)


def _retention(rq, rk, rv, rg, norm_g, batch, seq):
    n = batch * seq
    dmat, qdec, kdec, cdec = (jnp.asarray(a) for a in _ret_tables())
    row = lambda b: (b, 0)
    return pl.pallas_call(
        _ret_kernel,
        grid=(batch,),
        in_specs=[
            pl.BlockSpec((seq, RET_Q_COLS), row),
            pl.BlockSpec((seq, RET_K_COLS), row),
            pl.BlockSpec((seq, RET_WIDTH), row),
            pl.BlockSpec((seq, RET_WIDTH), row),
            _const_spec(dmat.shape),
            _const_spec(qdec.shape),
            _const_spec(kdec.shape),
            _const_spec(cdec.shape),
            _const_spec((1, RET_V_DIM)),
        ],
        out_specs=pl.BlockSpec((seq, RET_WIDTH), row),
        out_shape=jax.ShapeDtypeStruct((n, RET_WIDTH), BF16),
        compiler_params=pltpu.CompilerParams(
            dimension_semantics=("arbitrary",), vmem_limit_bytes=VMEM_LIMIT_BYTES),
        name="retention",
    )(rq, rk, rv, rg, dmat, qdec, kdec, cdec, norm_g)


def _ffn_kernel(x_ref, oa_ref, or_ref, wo_ref, g1_ref, g2_ref, wg_ref, wu_ref, wd_ref, g3_ref,
                out_ref):
    mix = (jnp.dot(oa_ref[...], wo_ref[:DA_WIDTH, :], preferred_element_type=F32)
           + jnp.dot(or_ref[...], wo_ref[DA_WIDTH:, :], preferred_element_type=F32))
    x1 = x_ref[...] + _rms(mix) * g1_ref[...]
    h = (_rms(x1) * g2_ref[...]).astype(BF16)
    gate = jnp.dot(h, wg_ref[...], preferred_element_type=F32)
    up = jnp.dot(h, wu_ref[...], preferred_element_type=F32)
    f = (gate * (1.0 / (1.0 + jnp.exp(-gate))) * up).astype(BF16)
    y = jnp.dot(f, wd_ref[...], preferred_element_type=F32)
    out_ref[...] = x1 + _rms(y) * g3_ref[...]


def _ffn(x2, oa, orr, wo, g1, g2, wg, wu, wd, g3):
    n = x2.shape[0]
    tm = ROW_TILE
    row = lambda i: (i, 0)
    return pl.pallas_call(
        _ffn_kernel,
        grid=(n // tm,),
        in_specs=[
            pl.BlockSpec((tm, D_MODEL), row),
            pl.BlockSpec((tm, DA_WIDTH), row),
            pl.BlockSpec((tm, RET_WIDTH), row),
            _const_spec(wo.shape),
            _const_spec((1, D_MODEL)),
            _const_spec((1, D_MODEL)),
            _const_spec(wg.shape),
            _const_spec(wu.shape),
            _const_spec(wd.shape),
            _const_spec((1, D_MODEL)),
        ],
        out_specs=pl.BlockSpec((tm, D_MODEL), row),
        out_shape=jax.ShapeDtypeStruct((n, D_MODEL), F32),
        compiler_params=pltpu.CompilerParams(
            dimension_semantics=("arbitrary",), vmem_limit_bytes=VMEM_LIMIT_BYTES),
        name="outproj_ffn",
    )(x2, oa, orr, wo, g1, g2, wg, wu, wd, g3)


def kernel(x, pre_mix_g, w_in, lambda_q1, lambda_k1, lambda_q2, lambda_k2, da_subln_g, ret_norm_g,
           w_out, post_mix_g, pre_ffn_g, w_gate, w_up, w_down, post_ffn_g):
    batch, seq, d = x.shape
    depth = w_in.shape[0]
    assert d == D_MODEL and seq % ATT_QTILE == 0 and seq % RET_BLOCK == 0
    assert ATT_QTILE == 2 * ATT_TILE and ATT_TILE % CHUNK == 0 and RET_BLOCK % CHUNK == 0
    assert (batch * seq) % ROW_TILE == 0 and ROW_TILE % ATT_TILE == 0
    x2 = x.reshape(batch * seq, d)
    c1 = DA_Q_COLS
    c2 = c1 + DA_K_COLS
    c3 = c2 + DA_V_COLS
    for l in range(depth):
        w = w_in[l].astype(BF16)
        qT, k, vT, rq, rk, rv, rg = _inproj(
            x2, pre_mix_g[l][None, :], w[:, :c1].T, w[:, c1:c2], w[:, c2:c3].T, w[:, c3:])
        lam_init = 0.8 - 0.6 * math.exp(-0.3 * l)
        lamv = jnp.stack([lambda_q1[l], lambda_k1[l], lambda_q2[l], lambda_k2[l]]).astype(F32)
        o_da = _attention(qT, k, vT, lamv, da_subln_g[l][None, :].astype(F32), batch, seq, lam_init)
        o_ret = _retention(rq, rk, rv, rg, ret_norm_g[l][None, :].astype(F32), batch, seq)
        x2 = _ffn(x2, o_da, o_ret, w_out[l].astype(BF16), post_mix_g[l][None, :],
                  pre_ffn_g[l][None, :], w_gate[l].astype(BF16), w_up[l].astype(BF16),
                  w_down[l].astype(BF16), post_ffn_g[l][None, :])
    return x2.reshape(batch, seq, d)
```

```python
import functools
import math

import numpy as np
import jax
import jax.numpy as jnp
from jax import lax
from jax.experimental import pallas as pl
from jax.experimental.pallas import tpu as pltpu

F32 = jnp.float32
BF16 = jnp.bfloat16

D_MODEL = 1024
CHUNK = 64
NORM_EPS = 1e-6
DA_HEADS = 4
DA_QK_DIM = 64
DA_V_DIM = 128
DA_WIDTH = DA_HEADS * DA_V_DIM
ALIBI_SLOPES = tuple(2.0 ** (-8.0 * (h + 1) / DA_HEADS) for h in range(DA_HEADS))
RET_HEADS = 4
RET_QK_DIM = 64
RET_V_DIM = 128
RET_WIDTH = RET_HEADS * RET_V_DIM
RET_GAMMA = tuple(1.0 - 2.0 ** (-5.0 - h) for h in range(RET_HEADS))
DA_Q_COLS = DA_HEADS * 2 * DA_QK_DIM
DA_K_COLS = DA_HEADS * 2 * DA_QK_DIM
DA_V_COLS = DA_WIDTH
RET_Q_COLS = RET_HEADS * RET_QK_DIM
RET_K_COLS = RET_HEADS * RET_QK_DIM

V7X_LANES = 128
V7X_VMEM_BYTES = 64 * 1024 * 1024
VMEM_LIMIT_BYTES = V7X_VMEM_BYTES - 8 * 1024 * 1024

INPROJ_TILE = 1024
ROW_TILE = 1024
FFN_SUB = 256
ATT_TILE = 256
ATT_QTILE = 512
ATT_DEN_ROWS = 16
ATT_GROUP = 1
ATT_LOOKAHEAD = 4
RET_BLOCK = 256
RET_LOOKAHEAD = 2
LOG2E = math.log2(math.e)
POS_TERMS = 3

_NT = (((1,), (1,)), ((), ()))
_TN = (((0,), (0,)), ((), ()))


def _rms(x):
    return x * lax.rsqrt(jnp.mean(x * x, axis=-1, keepdims=True) + NORM_EPS)


def _const_spec(shape):
    nd = len(shape)
    return pl.BlockSpec(shape, lambda *_: (0,) * nd, pipeline_mode=pl.Buffered(1))


def _inproj_kernel(x_ref, g_ref, wqT_ref, wk_ref, wvT_ref, wr_ref,
                   qT_ref, k_ref, vT_ref, rq_ref, rk_ref, rv_ref, rg_ref):
    for j in range(INPROJ_TILE // ATT_TILE):
        rows = slice(j * ATT_TILE, (j + 1) * ATT_TILE)
        h = (_rms(x_ref[rows, :]) * g_ref[...]).astype(BF16)
        qT = lax.dot_general(wqT_ref[...], h, _NT, preferred_element_type=F32)
        qT_ref[:, rows] = (qT * (DA_QK_DIM ** -0.5 * LOG2E)).astype(BF16)
        k = jnp.dot(h, wk_ref[...], preferred_element_type=F32)
        for hh in range(DA_HEADS):
            k_ref[hh, rows, :] = k[:, hh * 128:(hh + 1) * 128].astype(BF16)
        vT = lax.dot_general(wvT_ref[...], h, _NT, preferred_element_type=F32)
        vT_ref[j] = vT.astype(BF16)
        r = jnp.dot(h, wr_ref[...], preferred_element_type=F32)
        rq_ref[rows, :] = r[:, 0:256].astype(BF16)
        rk_ref[rows, :] = r[:, 256:512].astype(BF16)
        rv_ref[rows, :] = r[:, 512:1024].astype(BF16)
        rg_ref[rows, :] = r[:, 1024:1536].astype(BF16)


def _inproj(x2, g, wqT, wk, wvT, wr):
    n = x2.shape[0]
    tm = INPROJ_TILE
    row = lambda i: (i, 0)
    out_shapes = (
        jax.ShapeDtypeStruct((DA_Q_COLS, n), BF16),
        jax.ShapeDtypeStruct((DA_HEADS, n, 128), BF16),
        jax.ShapeDtypeStruct((n // ATT_TILE, DA_V_COLS, ATT_TILE), BF16),
        jax.ShapeDtypeStruct((n, RET_Q_COLS), BF16),
        jax.ShapeDtypeStruct((n, RET_K_COLS), BF16),
        jax.ShapeDtypeStruct((n, RET_WIDTH), BF16),
        jax.ShapeDtypeStruct((n, RET_WIDTH), BF16),
    )
    out_specs = (
        pl.BlockSpec((DA_Q_COLS, tm), lambda i: (0, i)),
        pl.BlockSpec((DA_HEADS, tm, 128), lambda i: (0, i, 0)),
        pl.BlockSpec((tm // ATT_TILE, DA_V_COLS, ATT_TILE), lambda i: (i, 0, 0)),
        pl.BlockSpec((tm, RET_Q_COLS), row),
        pl.BlockSpec((tm, RET_K_COLS), row),
        pl.BlockSpec((tm, RET_WIDTH), row),
        pl.BlockSpec((tm, RET_WIDTH), row),
    )
    return pl.pallas_call(
        _inproj_kernel,
        grid=(n // tm,),
        in_specs=[
            pl.BlockSpec((tm, D_MODEL), row),
            _const_spec((1, D_MODEL)),
            _const_spec(wqT.shape),
            _const_spec(wk.shape),
            _const_spec(wvT.shape),
            _const_spec(wr.shape),
        ],
        out_specs=out_specs,
        out_shape=out_shapes,
        compiler_params=pltpu.CompilerParams(
            dimension_semantics=("arbitrary",), vmem_limit_bytes=VMEM_LIMIT_BYTES),
        name="inproj",
    )(x2, g, wqT, wk, wvT, wr)


def _attn_tables():
    t = ATT_TILE
    kk = np.arange(t)[:, None]
    qq = np.arange(t)[None, :]
    allowed = (kk // CHUNK) <= (qq // CHUNK)
    diag = np.empty((DA_HEADS, t, 2 * t), np.float32)
    prow = np.zeros((DA_HEADS, 128, 2 * ATT_QTILE), np.float32)
    for h, slope in enumerate(ALIBI_SLOPES):
        c = slope * LOG2E
        d = np.where(allowed, -2.0 * c * np.maximum(kk - qq, 0), -np.inf)
        diag[h] = np.concatenate([d, d], axis=1)
        rem = c
        for j in range(POS_TERMS):
            piece = float(np.asarray(rem, np.float32).astype(jnp.bfloat16).astype(np.float32))
            prow[h, j, :] = piece * CHUNK
            prow[h, POS_TERMS + j, :] = piece
            rem = rem - piece
    return diag, prow


def _pos_features(s):
    pos = np.arange(s)
    feat = np.zeros((s, 128), np.float32)
    for j in range(POS_TERMS):
        feat[:, j] = pos // CHUNK
        feat[:, POS_TERMS + j] = pos % CHUNK
    return feat


def _attn_kernel(lam_init, qT_ref, k_ref, pos_ref, vT_ref, diag_ref, prow_ref, lamv_ref, g_ref,
                 o_ref, acc_ref):
    t = ATT_TILE
    tq = ATT_QTILE
    w = 2 * tq
    hw = w // 2
    nq = qT_ref.shape[1] // tq
    z = jnp.zeros((DA_QK_DIM, t), BF16)
    ones_rows = (lax.broadcasted_iota(jnp.int32, (ATT_DEN_ROWS, t), 0) == 0).astype(BF16)
    diag = diag_ref[0]
    lv = lamv_ref[...]
    lam = (jnp.exp(jnp.sum(lv[0:1] * lv[1:2], axis=-1, keepdims=True))
           - jnp.exp(jnp.sum(lv[2:3] * lv[3:4], axis=-1, keepdims=True)) + lam_init)

    def keys(k0, nk):
        rows = slice(k0 * t, (k0 + nk) * t)
        return jnp.concatenate([k_ref[rows, :], pos_ref[rows, :]], axis=1)

    def values_t(k0, nk):
        return jnp.concatenate(
            [jnp.concatenate([vT_ref[ki], ones_rows], axis=0) for ki in range(k0, k0 + nk)], axis=1)

    def colmax(a):
        return jnp.max(a, axis=0, keepdims=True)

    def prob(a, m):
        return jnp.exp2(a - m).astype(BF16)

    def combine(a):
        num = a[:DA_V_DIM] * (1.0 / a[DA_V_DIM:DA_V_DIM + 1])
        return num[:, :t] - lam * num[:, t:]

    def query_operand(qi):
        cols = []
        for half in range(2):
            qh = qT_ref[:, qi * tq + half * t:qi * tq + (half + 1) * t]
            cols.append(jnp.concatenate([qh[:DA_QK_DIM], z], axis=0))
            cols.append(jnp.concatenate([z, qh[DA_QK_DIM:]], axis=0))
        return jnp.concatenate([jnp.concatenate(cols, axis=1), prow_ref[0]], axis=0)

    tasks = []
    for qi in range(nq):
        tasks.append((qi, "first", 2 * qi))
        tasks += [(qi, "full", k0) for k0 in range(0, 2 * qi, ATT_GROUP)]
        tasks.append((qi, "last", 2 * qi + 1))

    qexts = {}

    def scores(task):
        qi, kind, k0 = task
        if qi not in qexts:
            qexts[qi] = query_operand(qi)
        qext = qexts[qi]
        if kind == "last":
            return jnp.dot(keys(k0, 1), qext[:, hw:], preferred_element_type=F32)
        return jnp.dot(keys(k0, ATT_GROUP if kind == "full" else 1), qext,
                       preferred_element_type=F32)

    pending = [scores(task) for task in tasks[:ATT_LOOKAHEAD]]
    m = None
    for idx, task in enumerate(tasks):
        qi, kind, k0 = task
        s = pending.pop(0)
        if idx + ATT_LOOKAHEAD < len(tasks):
            pending.append(scores(tasks[idx + ATT_LOOKAHEAD]))
        acc = acc_ref.at[qi % 2]
        if kind == "first":
            s = jnp.concatenate([s[:, :hw] + diag, s[:, hw:]], axis=1)
            m = colmax(s)
            acc[...] = jnp.dot(values_t(k0, 1), prob(s, m), preferred_element_type=F32)
        elif kind == "full":
            m_new = jnp.maximum(m, colmax(s))
            pv = jnp.dot(values_t(k0, ATT_GROUP), prob(s, m_new), preferred_element_type=F32)
            acc[...] = acc[...] * jnp.exp2(m - m_new) + pv
            m = m_new
        else:
            s = s + diag
            mh = m[:, hw:]
            mh_new = jnp.maximum(mh, colmax(s))
            pv = jnp.dot(values_t(k0, 1), prob(s, mh_new), preferred_element_type=F32)
            acc_h = acc[:, hw:] * jnp.exp2(mh - mh_new) + pv
            o = jnp.concatenate([combine(acc[:, :hw]), combine(acc_h)], axis=1)
            o = o * lax.rsqrt(jnp.mean(o * o, axis=0, keepdims=True) + NORM_EPS)
            o_ref[qi * tq:(qi + 1) * tq, :] = (
                o.T * g_ref[...] * (1.0 - lam_init)).astype(o_ref.dtype)


def _attention(qT, k, vT, lamv, subln_g, batch, seq, lam_init):
    t = ATT_TILE
    tq = ATT_QTILE
    nq = seq // tq
    nk = seq // t
    n = batch * seq
    diag, prow = _attn_tables()
    diag = jnp.asarray(diag)
    prow = jnp.asarray(prow, BF16)
    pos = jnp.asarray(_pos_features(seq), BF16)
    return pl.pallas_call(
        functools.partial(_attn_kernel, lam_init),
        grid=(batch, DA_HEADS),
        in_specs=[
            pl.BlockSpec((128, seq), lambda b, h: (h, b)),
            pl.BlockSpec((None, seq, 128), lambda b, h: (h, b, 0)),
            pl.BlockSpec((seq, 128), lambda b, h: (0, 0)),
            pl.BlockSpec((nk, 128, t), lambda b, h: (b, h, 0)),
            pl.BlockSpec((1, t, 2 * t), lambda b, h: (h, 0, 0)),
            pl.BlockSpec((1, 128, 2 * tq), lambda b, h: (h, 0, 0)),
            pl.BlockSpec((4, DA_QK_DIM), lambda b, h: (0, 0)),
            pl.BlockSpec((1, DA_V_DIM), lambda b, h: (0, 0)),
        ],
        out_specs=pl.BlockSpec((seq, DA_V_DIM), lambda b, h: (b, h)),
        out_shape=jax.ShapeDtypeStruct((n, DA_WIDTH), BF16),
        scratch_shapes=[pltpu.VMEM((2, DA_V_DIM + ATT_DEN_ROWS, 2 * tq), F32)],
        compiler_params=pltpu.CompilerParams(
            dimension_semantics=("arbitrary", "arbitrary"),
            vmem_limit_bytes=VMEM_LIMIT_BYTES),
        name="diff_attention",
    )(qT, k, pos, vT, diag, prow, lamv, subln_g)


def _ret_tables():
    L = RET_BLOCK
    i = np.arange(L)[:, None].astype(np.float64)
    j = np.arange(L)[None, :].astype(np.float64)
    allowed = (j // CHUNK) <= (i // CHUNK)
    dmat = np.empty((RET_HEADS, L, L), np.float32)
    qdec = np.zeros((RET_HEADS, L, 128), np.float32)
    kdec = np.zeros((RET_HEADS, L, 128), np.float32)
    cdec = np.empty((RET_HEADS, 1, 128), np.float32)
    lane = np.arange(128)[None, :]
    for h, gamma in enumerate(RET_GAMMA):
        mine = (lane // RET_QK_DIM) == (h % 2)
        dmat[h] = np.where(allowed, gamma ** np.abs(i - j), 0.0) * RET_QK_DIM ** -0.5
        qdec[h] = np.where(mine, gamma ** (i + 1.0), 0.0)
        kdec[h] = np.where(mine, gamma ** (L - 1.0 - i), 0.0) * RET_QK_DIM ** -0.5
        cdec[h] = gamma ** L
    return dmat, qdec, kdec, cdec


def _ret_kernel(q_ref, k_ref, v_ref, gate_ref, dmat_ref, qdec_ref, kdec_ref, cdec_ref, g_ref,
                o_ref):
    L = RET_BLOCK
    tasks = [(blk, h) for blk in range(q_ref.shape[0] // L) for h in range(RET_HEADS)]

    def operands(task):
        blk, h = task
        rows = slice(blk * L, (blk + 1) * L)
        pair = slice((h // 2) * 128, (h // 2 + 1) * 128)
        return rows, slice(h * RET_V_DIM, (h + 1) * RET_V_DIM), q_ref[rows, pair], k_ref[rows, pair]

    def scores(task):
        _, h = task
        _, _, q, k = operands(task)
        qm = jnp.where(qdec_ref[h] > 0.0, q, jnp.zeros_like(q))
        s = lax.dot_general(qm, k, _NT, preferred_element_type=F32) * dmat_ref[h]
        return s.astype(BF16)

    states = [jnp.zeros((128, RET_V_DIM), F32) for _ in range(RET_HEADS)]
    pending = [scores(task) for task in tasks[:RET_LOOKAHEAD]]
    for idx, task in enumerate(tasks):
        _, h = task
        s = pending.pop(0)
        if idx + RET_LOOKAHEAD < len(tasks):
            pending.append(scores(tasks[idx + RET_LOOKAHEAD]))
        rows, head, q, k = operands(task)
        v = v_ref[rows, head]
        inner = jnp.dot(s, v, preferred_element_type=F32)
        qd = (q.astype(F32) * qdec_ref[h]).astype(BF16)
        cross = jnp.dot(qd, states[h].astype(BF16), preferred_element_type=F32)
        kd = (k.astype(F32) * kdec_ref[h]).astype(BF16)
        states[h] = states[h] * cdec_ref[h] + lax.dot_general(
            kd, v, _TN, preferred_element_type=F32)
        o = _rms(inner + cross) * g_ref[...]
        gate = gate_ref[rows, head].astype(F32)
        o_ref[rows, head] = (gate * (1.0 / (1.0 + jnp.exp(-gate))) * o).astype(o_ref.dtype)


def _retention(rq, rk, rv, rg, norm_g, batch, seq):
    n = batch * seq
    dmat, qdec, kdec, cdec = (jnp.asarray(a) for a in _ret_tables())
    row = lambda b: (b, 0)
    return pl.pallas_call(
        _ret_kernel,
        grid=(batch,),
        in_specs=[
            pl.BlockSpec((seq, RET_Q_COLS), row),
            pl.BlockSpec((seq, RET_K_COLS), row),
            pl.BlockSpec((seq, RET_WIDTH), row),
            pl.BlockSpec((seq, RET_WIDTH), row),
            _const_spec(dmat.shape),
            _const_spec(qdec.shape),
            _const_spec(kdec.shape),
            _const_spec(cdec.shape),
            _const_spec((1, RET_V_DIM)),
        ],
        out_specs=pl.BlockSpec((seq, RET_WIDTH), row),
        out_shape=jax.ShapeDtypeStruct((n, RET_WIDTH), BF16),
        compiler_params=pltpu.CompilerParams(
            dimension_semantics=("arbitrary",), vmem_limit_bytes=VMEM_LIMIT_BYTES),
        name="retention",
    )(rq, rk, rv, rg, dmat, qdec, kdec, cdec, norm_g)


def _ffn_kernel(x_ref, oa_ref, or_ref, wo_ref, g1_ref, g2_ref, wg_ref, wu_ref, wd_ref, g3_ref,
                out_ref):
    subs = [slice(j * FFN_SUB, (j + 1) * FFN_SUB) for j in range(x_ref.shape[0] // FFN_SUB)]
    mix = [jnp.dot(oa_ref[r, :], wo_ref[:DA_WIDTH, :], preferred_element_type=F32)
           + jnp.dot(or_ref[r, :], wo_ref[DA_WIDTH:, :], preferred_element_type=F32) for r in subs]
    x1 = [x_ref[r, :] + _rms(mx) * g1_ref[...] for r, mx in zip(subs, mix)]
    h = [(_rms(a) * g2_ref[...]).astype(BF16) for a in x1]
    gate = [jnp.dot(a, wg_ref[...], preferred_element_type=F32) for a in h]
    up = [jnp.dot(a, wu_ref[...], preferred_element_type=F32) for a in h]
    f = [(gt * (1.0 / (1.0 + jnp.exp(-gt))) * u).astype(BF16) for gt, u in zip(gate, up)]
    y = [jnp.dot(a, wd_ref[...], preferred_element_type=F32) for a in f]
    for r, a, b in zip(subs, x1, y):
        out_ref[r, :] = a + _rms(b) * g3_ref[...]


def _ffn(x2, oa, orr, wo, g1, g2, wg, wu, wd, g3):
    n = x2.shape[0]
    tm = ROW_TILE
    row = lambda i: (i, 0)
    return pl.pallas_call(
        _ffn_kernel,
        grid=(n // tm,),
        in_specs=[
            pl.BlockSpec((tm, D_MODEL), row),
            pl.BlockSpec((tm, DA_WIDTH), row),
            pl.BlockSpec((tm, RET_WIDTH), row),
            _const_spec(wo.shape),
            _const_spec((1, D_MODEL)),
            _const_spec((1, D_MODEL)),
            _const_spec(wg.shape),
            _const_spec(wu.shape),
            _const_spec(wd.shape),
            _const_spec((1, D_MODEL)),
        ],
        out_specs=pl.BlockSpec((tm, D_MODEL), row),
        out_shape=jax.ShapeDtypeStruct((n, D_MODEL), F32),
        compiler_params=pltpu.CompilerParams(
            dimension_semantics=("arbitrary",), vmem_limit_bytes=VMEM_LIMIT_BYTES),
        name="outproj_ffn",
    )(x2, oa, orr, wo, g1, g2, wg, wu, wd, g3)


def kernel(x, pre_mix_g, w_in, lambda_q1, lambda_k1, lambda_q2, lambda_k2, da_subln_g, ret_norm_g,
           w_out, post_mix_g, pre_ffn_g, w_gate, w_up, w_down, post_ffn_g):
    batch, seq, d = x.shape
    depth = w_in.shape[0]
    assert d == D_MODEL and seq % ATT_QTILE == 0 and seq % RET_BLOCK == 0
    assert ATT_QTILE == 2 * ATT_TILE and ATT_TILE % CHUNK == 0 and RET_BLOCK % CHUNK == 0
    assert (batch * seq) % ROW_TILE == 0 and ROW_TILE % FFN_SUB == 0
    assert (batch * seq) % INPROJ_TILE == 0 and INPROJ_TILE % ATT_TILE == 0
    x2 = x.reshape(batch * seq, d)
    c1 = DA_Q_COLS
    c2 = c1 + DA_K_COLS
    c3 = c2 + DA_V_COLS
    for l in range(depth):
        w = w_in[l].astype(BF16)
        qT, k, vT, rq, rk, rv, rg = _inproj(
            x2, pre_mix_g[l][None, :], w[:, :c1].T, w[:, c1:c2], w[:, c2:c3].T, w[:, c3:])
        lam_init = 0.8 - 0.6 * math.exp(-0.3 * l)
        lamv = jnp.stack([lambda_q1[l], lambda_k1[l], lambda_q2[l], lambda_k2[l]]).astype(F32)
        o_da = _attention(qT, k, vT, lamv, da_subln_g[l][None, :].astype(F32), batch, seq, lam_init)
        o_ret = _retention(rq, rk, rv, rg, ret_norm_g[l][None, :].astype(F32), batch, seq)
        x2 = _ffn(x2, o_da, o_ret, w_out[l].astype(BF16), post_mix_g[l][None, :],
                  pre_ffn_g[l][None, :], w_gate[l].astype(BF16), w_up[l].astype(BF16),
                  w_down[l].astype(BF16), post_ffn_g[l][None, :])
    return x2.reshape(batch, seq, d)
```

```python
import functools
import math

import numpy as np
import jax
import jax.numpy as jnp
from jax import lax
from jax.experimental import pallas as pl
from jax.experimental.pallas import tpu as pltpu

F32 = jnp.float32
BF16 = jnp.bfloat16

D_MODEL = 1024
CHUNK = 64
NORM_EPS = 1e-6
DA_HEADS = 4
DA_QK_DIM = 64
DA_V_DIM = 128
DA_WIDTH = DA_HEADS * DA_V_DIM
ALIBI_SLOPES = tuple(2.0 ** (-8.0 * (h + 1) / DA_HEADS) for h in range(DA_HEADS))
RET_HEADS = 4
RET_QK_DIM = 64
RET_V_DIM = 128
RET_WIDTH = RET_HEADS * RET_V_DIM
RET_GAMMA = tuple(1.0 - 2.0 ** (-5.0 - h) for h in range(RET_HEADS))
DA_Q_COLS = DA_HEADS * 2 * DA_QK_DIM
DA_K_COLS = DA_HEADS * 2 * DA_QK_DIM
DA_V_COLS = DA_WIDTH
RET_Q_COLS = RET_HEADS * RET_QK_DIM
RET_K_COLS = RET_HEADS * RET_QK_DIM

V7X_LANES = 128
V7X_VMEM_BYTES = 64 * 1024 * 1024
VMEM_LIMIT_BYTES = V7X_VMEM_BYTES - 8 * 1024 * 1024

INPROJ_TILE = 1024
ROW_TILE = 1024
FFN_SUB = 256
ATT_TILE = 256
ATT_QTILE = 512
ATT_DEN_ROWS = 16
ATT_LOOKAHEAD = 3
ATT_EXP_ROWS = 32
RET_BLOCK = 256
RET_LOOKAHEAD = 2
LOG2E = math.log2(math.e)
POS_TERMS = 3

_NT = (((1,), (1,)), ((), ()))
_TN = (((0,), (0,)), ((), ()))


def _rms(x):
    return x * lax.rsqrt(jnp.mean(x * x, axis=-1, keepdims=True) + NORM_EPS)


def _const_spec(shape):
    nd = len(shape)
    return pl.BlockSpec(shape, lambda *_: (0,) * nd, pipeline_mode=pl.Buffered(1))


def _inproj_kernel(x_ref, g_ref, wqT_ref, wk_ref, wvT_ref, wr_ref,
                   qT_ref, k_ref, vT_ref, rq_ref, rk_ref, rv_ref, rg_ref):
    for j in range(INPROJ_TILE // ATT_TILE):
        rows = slice(j * ATT_TILE, (j + 1) * ATT_TILE)
        h = (_rms(x_ref[rows, :]) * g_ref[...]).astype(BF16)
        qT = lax.dot_general(wqT_ref[...], h, _NT, preferred_element_type=F32)
        qT_ref[:, rows] = (qT * (DA_QK_DIM ** -0.5 * LOG2E)).astype(BF16)
        k = jnp.dot(h, wk_ref[...], preferred_element_type=F32)
        for hh in range(DA_HEADS):
            k_ref[hh, rows, :] = k[:, hh * 128:(hh + 1) * 128].astype(BF16)
        vT = lax.dot_general(wvT_ref[...], h, _NT, preferred_element_type=F32)
        vT_ref[j] = vT.astype(BF16)
        r = jnp.dot(h, wr_ref[...], preferred_element_type=F32)
        rq_ref[rows, :] = r[:, 0:256].astype(BF16)
        rk_ref[rows, :] = r[:, 256:512].astype(BF16)
        rv_ref[rows, :] = r[:, 512:1024].astype(BF16)
        rg_ref[rows, :] = r[:, 1024:1536].astype(BF16)


def _inproj(x2, g, wqT, wk, wvT, wr):
    n = x2.shape[0]
    tm = INPROJ_TILE
    row = lambda i: (i, 0)
    out_shapes = (
        jax.ShapeDtypeStruct((DA_Q_COLS, n), BF16),
        jax.ShapeDtypeStruct((DA_HEADS, n, 128), BF16),
        jax.ShapeDtypeStruct((n // ATT_TILE, DA_V_COLS, ATT_TILE), BF16),
        jax.ShapeDtypeStruct((n, RET_Q_COLS), BF16),
        jax.ShapeDtypeStruct((n, RET_K_COLS), BF16),
        jax.ShapeDtypeStruct((n, RET_WIDTH), BF16),
        jax.ShapeDtypeStruct((n, RET_WIDTH), BF16),
    )
    out_specs = (
        pl.BlockSpec((DA_Q_COLS, tm), lambda i: (0, i)),
        pl.BlockSpec((DA_HEADS, tm, 128), lambda i: (0, i, 0)),
        pl.BlockSpec((tm // ATT_TILE, DA_V_COLS, ATT_TILE), lambda i: (i, 0, 0)),
        pl.BlockSpec((tm, RET_Q_COLS), row),
        pl.BlockSpec((tm, RET_K_COLS), row),
        pl.BlockSpec((tm, RET_WIDTH), row),
        pl.BlockSpec((tm, RET_WIDTH), row),
    )
    return pl.pallas_call(
        _inproj_kernel,
        grid=(n // tm,),
        in_specs=[
            pl.BlockSpec((tm, D_MODEL), row),
            _const_spec((1, D_MODEL)),
            _const_spec(wqT.shape),
            _const_spec(wk.shape),
            _const_spec(wvT.shape),
            _const_spec(wr.shape),
        ],
        out_specs=out_specs,
        out_shape=out_shapes,
        compiler_params=pltpu.CompilerParams(
            dimension_semantics=("arbitrary",), vmem_limit_bytes=VMEM_LIMIT_BYTES),
        name="inproj",
    )(x2, g, wqT, wk, wvT, wr)


def _attn_tables():
    t = ATT_TILE
    kk = np.arange(t)[:, None]
    qq = np.arange(t)[None, :]
    allowed = (kk // CHUNK) <= (qq // CHUNK)
    diag = np.empty((DA_HEADS, t, 2 * t), np.float32)
    prow = np.zeros((DA_HEADS, 128, 2 * ATT_QTILE), np.float32)
    for h, slope in enumerate(ALIBI_SLOPES):
        c = slope * LOG2E
        d = np.where(allowed, -2.0 * c * np.maximum(kk - qq, 0), -np.inf)
        diag[h] = np.concatenate([d, d], axis=1)
        rem = c
        for j in range(POS_TERMS):
            piece = float(np.asarray(rem, np.float32).astype(jnp.bfloat16).astype(np.float32))
            prow[h, j, :] = piece * CHUNK
            prow[h, POS_TERMS + j, :] = piece
            rem = rem - piece
    return diag, prow


def _pos_features(s):
    pos = np.arange(s)
    feat = np.zeros((s, 128), np.float32)
    for j in range(POS_TERMS):
        feat[:, j] = pos // CHUNK
        feat[:, POS_TERMS + j] = pos % CHUNK
    return feat


def _attn_kernel(lam_init, qT_ref, k_ref, pos_ref, vT_ref, diag_ref, prow_ref, lamv_ref, g_ref,
                 o_ref, acc_ref, *sp_refs):
    t = ATT_TILE
    tq = ATT_QTILE
    w = 2 * tq
    hw = w // 2
    nq = qT_ref.shape[1] // tq
    s_refs = sp_refs[:ATT_LOOKAHEAD + 1]
    p_refs = sp_refs[ATT_LOOKAHEAD + 1:]
    z = jnp.zeros((DA_QK_DIM, t), BF16)
    ones_rows = (lax.broadcasted_iota(jnp.int32, (ATT_DEN_ROWS, t), 0) == 0).astype(BF16)
    diag = diag_ref[0]
    lv = lamv_ref[...]
    lam = (jnp.exp(jnp.sum(lv[0:1] * lv[1:2], axis=-1, keepdims=True))
           - jnp.exp(jnp.sum(lv[2:3] * lv[3:4], axis=-1, keepdims=True)) + lam_init)

    def keys(k0, nk):
        rows = slice(k0 * t, (k0 + nk) * t)
        return jnp.concatenate([k_ref[rows, :], pos_ref[rows, :]], axis=1)

    def values_t(k0, nk):
        return jnp.concatenate(
            [jnp.concatenate([vT_ref[ki], ones_rows], axis=0) for ki in range(k0, k0 + nk)], axis=1)

    def colmax(a):
        return jnp.max(a, axis=0, keepdims=True)

    def prob(a, m):
        return jnp.exp2(a - m).astype(BF16)

    def combine(a):
        num = a[:DA_V_DIM] * (1.0 / a[DA_V_DIM:DA_V_DIM + 1])
        return num[:, :t] - lam * num[:, t:]

    def query_operand(qi):
        cols = []
        for half in range(2):
            qh = qT_ref[:, qi * tq + half * t:qi * tq + (half + 1) * t]
            cols.append(jnp.concatenate([qh[:DA_QK_DIM], z], axis=0))
            cols.append(jnp.concatenate([z, qh[DA_QK_DIM:]], axis=0))
        return jnp.concatenate([jnp.concatenate(cols, axis=1), prow_ref[0]], axis=0)

    tasks = []
    for qi in range(nq):
        tasks.append((qi, "first", 2 * qi))
        tasks += [(qi, "full", k0) for k0 in range(2 * qi)]
        tasks.append((qi, "last", 2 * qi + 1))

    qexts = {}

    def scores(task):
        qi, kind, k0 = task
        if qi not in qexts:
            qexts[qi] = query_operand(qi)
        qext = qexts[qi]
        if kind == "last":
            s = jnp.dot(keys(k0, 1), qext[:, hw:], preferred_element_type=F32) + diag
        else:
            s = jnp.dot(keys(k0, 1), qext, preferred_element_type=F32)
            if kind == "first":
                s = jnp.concatenate([s[:, :hw] + diag, s[:, hw:]], axis=1)
        return s

    row0 = pl.multiple_of(jnp.minimum(pl.program_id(0), 0), t)

    def issue(idx):
        s = scores(tasks[idx])
        s_refs[idx % len(s_refs)][pl.ds(row0, t), :s.shape[1]] = s
        return colmax(s)

    def probabilities(idx, m):
        ncol = hw if tasks[idx][1] == "last" else w
        s_ref = s_refs[idx % len(s_refs)]
        p_ref = p_refs[idx % len(p_refs)]
        for r in range(0, t, ATT_EXP_ROWS):
            rows = pl.ds(pl.multiple_of(row0 + r, ATT_EXP_ROWS), ATT_EXP_ROWS)
            p_ref[rows, :ncol] = prob(s_ref[rows, :ncol], m)
        return p_ref[pl.ds(row0, t), :ncol]

    pending = [issue(idx) for idx in range(ATT_LOOKAHEAD)]
    m = None
    for idx, task in enumerate(tasks):
        qi, kind, k0 = task
        if idx + ATT_LOOKAHEAD < len(tasks):
            pending.append(issue(idx + ATT_LOOKAHEAD))
        smax = pending.pop(0)
        acc = acc_ref.at[qi % 2]
        if kind == "first":
            m = smax
            acc[...] = jnp.dot(values_t(k0, 1), probabilities(idx, m), preferred_element_type=F32)
        elif kind == "full":
            m_new = jnp.maximum(m, smax)
            pv = jnp.dot(values_t(k0, 1), probabilities(idx, m_new), preferred_element_type=F32)
            acc[...] = acc[...] * jnp.exp2(m - m_new) + pv
            m = m_new
        else:
            mh = m[:, hw:]
            mh_new = jnp.maximum(mh, smax)
            pv = jnp.dot(values_t(k0, 1), probabilities(idx, mh_new), preferred_element_type=F32)
            acc_h = acc[:, hw:] * jnp.exp2(mh - mh_new) + pv
            o = jnp.concatenate([combine(acc[:, :hw]), combine(acc_h)], axis=1)
            o = o * lax.rsqrt(jnp.mean(o * o, axis=0, keepdims=True) + NORM_EPS)
            o_ref[qi * tq:(qi + 1) * tq, :] = (
                o.T * g_ref[...] * (1.0 - lam_init)).astype(o_ref.dtype)


def _attention(qT, k, vT, lamv, subln_g, batch, seq, lam_init):
    t = ATT_TILE
    tq = ATT_QTILE
    nq = seq // tq
    nk = seq // t
    n = batch * seq
    diag, prow = _attn_tables()
    diag = jnp.asarray(diag)
    prow = jnp.asarray(prow, BF16)
    pos = jnp.asarray(_pos_features(seq), BF16)
    return pl.pallas_call(
        functools.partial(_attn_kernel, lam_init),
        grid=(batch, DA_HEADS),
        in_specs=[
            pl.BlockSpec((128, seq), lambda b, h: (h, b)),
            pl.BlockSpec((None, seq, 128), lambda b, h: (h, b, 0)),
            pl.BlockSpec((seq, 128), lambda b, h: (0, 0)),
            pl.BlockSpec((nk, 128, t), lambda b, h: (b, h, 0)),
            pl.BlockSpec((1, t, 2 * t), lambda b, h: (h, 0, 0)),
            pl.BlockSpec((1, 128, 2 * tq), lambda b, h: (h, 0, 0)),
            pl.BlockSpec((4, DA_QK_DIM), lambda b, h: (0, 0)),
            pl.BlockSpec((1, DA_V_DIM), lambda b, h: (0, 0)),
        ],
        out_specs=pl.BlockSpec((seq, DA_V_DIM), lambda b, h: (b, h)),
        out_shape=jax.ShapeDtypeStruct((n, DA_WIDTH), BF16),
        scratch_shapes=[
            pltpu.VMEM((2, DA_V_DIM + ATT_DEN_ROWS, 2 * tq), F32),
            *[pltpu.VMEM((t, 2 * tq), F32) for _ in range(ATT_LOOKAHEAD + 1)],
            *[pltpu.VMEM((t, 2 * tq), BF16) for _ in range(2)],
        ],
        compiler_params=pltpu.CompilerParams(
            dimension_semantics=("arbitrary", "arbitrary"),
            vmem_limit_bytes=VMEM_LIMIT_BYTES),
        name="diff_attention",
    )(qT, k, pos, vT, diag, prow, lamv, subln_g)


def _ret_tables():
    L = RET_BLOCK
    i = np.arange(L)[:, None].astype(np.float64)
    j = np.arange(L)[None, :].astype(np.float64)
    allowed = (j // CHUNK) <= (i // CHUNK)
    dmat = np.empty((RET_HEADS, L, L), np.float32)
    qdec = np.zeros((RET_HEADS, L, 128), np.float32)
    kdec = np.zeros((RET_HEADS, L, 128), np.float32)
    cdec = np.empty((RET_HEADS, 1, 128), np.float32)
    lane = np.arange(128)[None, :]
    for h, gamma in enumerate(RET_GAMMA):
        mine = (lane // RET_QK_DIM) == (h % 2)
        dmat[h] = np.where(allowed, gamma ** np.abs(i - j), 0.0) * RET_QK_DIM ** -0.5
        qdec[h] = np.where(mine, gamma ** (i + 1.0), 0.0)
        kdec[h] = np.where(mine, gamma ** (L - 1.0 - i), 0.0) * RET_QK_DIM ** -0.5
        cdec[h] = gamma ** L
    return dmat, qdec, kdec, cdec


def _ret_kernel(q_ref, k_ref, v_ref, gate_ref, dmat_ref, qdec_ref, kdec_ref, cdec_ref, g_ref,
                o_ref):
    L = RET_BLOCK
    tasks = [(blk, h) for blk in range(q_ref.shape[0] // L) for h in range(RET_HEADS)]

    def operands(task):
        blk, h = task
        rows = slice(blk * L, (blk + 1) * L)
        pair = slice((h // 2) * 128, (h // 2 + 1) * 128)
        return rows, slice(h * RET_V_DIM, (h + 1) * RET_V_DIM), q_ref[rows, pair], k_ref[rows, pair]

    def scores(task):
        _, h = task
        _, _, q, k = operands(task)
        qm = jnp.where(qdec_ref[h] > 0.0, q, jnp.zeros_like(q))
        s = lax.dot_general(qm, k, _NT, preferred_element_type=F32) * dmat_ref[h]
        return s.astype(BF16)

    states = [jnp.zeros((128, RET_V_DIM), F32) for _ in range(RET_HEADS)]
    pending = [scores(task) for task in tasks[:RET_LOOKAHEAD]]
    for idx, task in enumerate(tasks):
        _, h = task
        s = pending.pop(0)
        if idx + RET_LOOKAHEAD < len(tasks):
            pending.append(scores(tasks[idx + RET_LOOKAHEAD]))
        rows, head, q, k = operands(task)
        v = v_ref[rows, head]
        inner = jnp.dot(s, v, preferred_element_type=F32)
        qd = (q.astype(F32) * qdec_ref[h]).astype(BF16)
        cross = jnp.dot(qd, states[h].astype(BF16), preferred_element_type=F32)
        kd = (k.astype(F32) * kdec_ref[h]).astype(BF16)
        states[h] = states[h] * cdec_ref[h] + lax.dot_general(
            kd, v, _TN, preferred_element_type=F32)
        o = _rms(inner + cross) * g_ref[...]
        gate = gate_ref[rows, head].astype(F32)
        o_ref[rows, head] = (gate * (1.0 / (1.0 + jnp.exp(-gate))) * o).astype(o_ref.dtype)


def _retention(rq, rk, rv, rg, norm_g, batch, seq):
    n = batch * seq
    dmat, qdec, kdec, cdec = (jnp.asarray(a) for a in _ret_tables())
    row = lambda b: (b, 0)
    return pl.pallas_call(
        _ret_kernel,
        grid=(batch,),
        in_specs=[
            pl.BlockSpec((seq, RET_Q_COLS), row),
            pl.BlockSpec((seq, RET_K_COLS), row),
            pl.BlockSpec((seq, RET_WIDTH), row),
            pl.BlockSpec((seq, RET_WIDTH), row),
            _const_spec(dmat.shape),
            _const_spec(qdec.shape),
            _const_spec(kdec.shape),
            _const_spec(cdec.shape),
            _const_spec((1, RET_V_DIM)),
        ],
        out_specs=pl.BlockSpec((seq, RET_WIDTH), row),
        out_shape=jax.ShapeDtypeStruct((n, RET_WIDTH), BF16),
        compiler_params=pltpu.CompilerParams(
            dimension_semantics=("arbitrary",), vmem_limit_bytes=VMEM_LIMIT_BYTES),
        name="retention",
    )(rq, rk, rv, rg, dmat, qdec, kdec, cdec, norm_g)


def _ffn_kernel(x_ref, oa_ref, or_ref, wo_ref, g1_ref, g2_ref, wg_ref, wu_ref, wd_ref, g3_ref,
                out_ref):
    subs = [slice(j * FFN_SUB, (j + 1) * FFN_SUB) for j in range(x_ref.shape[0] // FFN_SUB)]
    mix = [jnp.dot(oa_ref[r, :], wo_ref[:DA_WIDTH, :], preferred_element_type=F32)
           + jnp.dot(or_ref[r, :], wo_ref[DA_WIDTH:, :], preferred_element_type=F32) for r in subs]
    x1 = [x_ref[r, :] + _rms(mx) * g1_ref[...] for r, mx in zip(subs, mix)]
    h = [(_rms(a) * g2_ref[...]).astype(BF16) for a in x1]
    gate = [jnp.dot(a, wg_ref[...], preferred_element_type=F32) for a in h]
    up = [jnp.dot(a, wu_ref[...], preferred_element_type=F32) for a in h]
    f = [(gt * (1.0 / (1.0 + jnp.exp(-gt))) * u).astype(BF16) for gt, u in zip(gate, up)]
    y = [jnp.dot(a, wd_ref[...], preferred_element_type=F32) for a in f]
    for r, a, b in zip(subs, x1, y):
        out_ref[r, :] = a + _rms(b) * g3_ref[...]


def _ffn(x2, oa, orr, wo, g1, g2, wg, wu, wd, g3):
    n = x2.shape[0]
    tm = ROW_TILE
    row = lambda i: (i, 0)
    return pl.pallas_call(
        _ffn_kernel,
        grid=(n // tm,),
        in_specs=[
            pl.BlockSpec((tm, D_MODEL), row),
            pl.BlockSpec((tm, DA_WIDTH), row),
            pl.BlockSpec((tm, RET_WIDTH), row),
            _const_spec(wo.shape),
            _const_spec((1, D_MODEL)),
            _const_spec((1, D_MODEL)),
            _const_spec(wg.shape),
            _const_spec(wu.shape),
            _const_spec(wd.shape),
            _const_spec((1, D_MODEL)),
        ],
        out_specs=pl.BlockSpec((tm, D_MODEL), row),
        out_shape=jax.ShapeDtypeStruct((n, D_MODEL), F32),
        compiler_params=pltpu.CompilerParams(
            dimension_semantics=("arbitrary",), vmem_limit_bytes=VMEM_LIMIT_BYTES),
        name="outproj_ffn",
    )(x2, oa, orr, wo, g1, g2, wg, wu, wd, g3)


def kernel(x, pre_mix_g, w_in, lambda_q1, lambda_k1, lambda_q2, lambda_k2, da_subln_g, ret_norm_g,
           w_out, post_mix_g, pre_ffn_g, w_gate, w_up, w_down, post_ffn_g):
    batch, seq, d = x.shape
    depth = w_in.shape[0]
    assert d == D_MODEL and seq % ATT_QTILE == 0 and seq % RET_BLOCK == 0
    assert ATT_QTILE == 2 * ATT_TILE and ATT_TILE % CHUNK == 0 and RET_BLOCK % CHUNK == 0
    assert (batch * seq) % ROW_TILE == 0 and ROW_TILE % FFN_SUB == 0
    assert (batch * seq) % INPROJ_TILE == 0 and INPROJ_TILE % ATT_TILE == 0
    x2 = x.reshape(batch * seq, d)
    c1 = DA_Q_COLS
    c2 = c1 + DA_K_COLS
    c3 = c2 + DA_V_COLS
    for l in range(depth):
        w = w_in[l].astype(BF16)
        qT, k, vT, rq, rk, rv, rg = _inproj(
            x2, pre_mix_g[l][None, :], w[:, :c1].T, w[:, c1:c2], w[:, c2:c3].T, w[:, c3:])
        lam_init = 0.8 - 0.6 * math.exp(-0.3 * l)
        lamv = jnp.stack([lambda_q1[l], lambda_k1[l], lambda_q2[l], lambda_k2[l]]).astype(F32)
        o_da = _attention(qT, k, vT, lamv, da_subln_g[l][None, :].astype(F32), batch, seq, lam_init)
        o_ret = _retention(rq, rk, rv, rg, ret_norm_g[l][None, :].astype(F32), batch, seq)
        x2 = _ffn(x2, o_da, o_ret, w_out[l].astype(BF16), post_mix_g[l][None, :],
                  pre_ffn_g[l][None, :], w_gate[l].astype(BF16), w_up[l].astype(BF16),
                  w_down[l].astype(BF16), post_ffn_g[l][None, :])
    return x2.reshape(batch, seq, d)
```

```python
import functools
import math

import numpy as np
import jax
import jax.numpy as jnp
from jax import lax
from jax.experimental import pallas as pl
from jax.experimental.pallas import tpu as pltpu

F32 = jnp.float32
BF16 = jnp.bfloat16

D_MODEL = 1024
CHUNK = 64
NORM_EPS = 1e-6
DA_HEADS = 4
DA_QK_DIM = 64
DA_V_DIM = 128
DA_WIDTH = DA_HEADS * DA_V_DIM
ALIBI_SLOPES = tuple(2.0 ** (-8.0 * (h + 1) / DA_HEADS) for h in range(DA_HEADS))
RET_HEADS = 4
RET_QK_DIM = 64
RET_V_DIM = 128
RET_WIDTH = RET_HEADS * RET_V_DIM
RET_GAMMA = tuple(1.0 - 2.0 ** (-5.0 - h) for h in range(RET_HEADS))
DA_Q_COLS = DA_HEADS * 2 * DA_QK_DIM
DA_K_COLS = DA_HEADS * 2 * DA_QK_DIM
DA_V_COLS = DA_WIDTH
RET_Q_COLS = RET_HEADS * RET_QK_DIM
RET_K_COLS = RET_HEADS * RET_QK_DIM

V7X_LANES = 128
V7X_VMEM_BYTES = 64 * 1024 * 1024
VMEM_LIMIT_BYTES = V7X_VMEM_BYTES - 8 * 1024 * 1024

INPROJ_TILE = 1024
ROW_TILE = 1024
FFN_SUB = 256
ATT_TILE = 256
ATT_QTILE = 512
ATT_DEN_ROWS = 16
ATT_GROUP = 2
ATT_LOOKAHEAD = 4
ATT_EXP_ROWS = 32
RET_BLOCK = 256
RET_LOOKAHEAD = 2
LOG2E = math.log2(math.e)
POS_TERMS = 3

_NT = (((1,), (1,)), ((), ()))
_TN = (((0,), (0,)), ((), ()))


def _rms(x):
    return x * lax.rsqrt(jnp.mean(x * x, axis=-1, keepdims=True) + NORM_EPS)


def _const_spec(shape):
    nd = len(shape)
    return pl.BlockSpec(shape, lambda *_: (0,) * nd, pipeline_mode=pl.Buffered(1))


def _inproj_kernel(x_ref, g_ref, wqT_ref, wk_ref, wvT_ref, wr_ref,
                   qT_ref, k_ref, vT_ref, rq_ref, rk_ref, rv_ref, rg_ref):
    for j in range(INPROJ_TILE // ATT_TILE):
        rows = slice(j * ATT_TILE, (j + 1) * ATT_TILE)
        h = (_rms(x_ref[rows, :]) * g_ref[...]).astype(BF16)
        qT = lax.dot_general(wqT_ref[...], h, _NT, preferred_element_type=F32)
        qT_ref[:, rows] = (qT * (DA_QK_DIM ** -0.5 * LOG2E)).astype(BF16)
        k = jnp.dot(h, wk_ref[...], preferred_element_type=F32)
        for hh in range(DA_HEADS):
            k_ref[hh, rows, :] = k[:, hh * 128:(hh + 1) * 128].astype(BF16)
        vT = lax.dot_general(wvT_ref[...], h, _NT, preferred_element_type=F32)
        vT_ref[j] = vT.astype(BF16)
        r = jnp.dot(h, wr_ref[...], preferred_element_type=F32)
        rq_ref[rows, :] = r[:, 0:256].astype(BF16)
        rk_ref[rows, :] = r[:, 256:512].astype(BF16)
        rv_ref[rows, :] = r[:, 512:1024].astype(BF16)
        rg_ref[rows, :] = r[:, 1024:1536].astype(BF16)


def _inproj(x2, g, wqT, wk, wvT, wr):
    n = x2.shape[0]
    tm = INPROJ_TILE
    row = lambda i: (i, 0)
    out_shapes = (
        jax.ShapeDtypeStruct((DA_Q_COLS, n), BF16),
        jax.ShapeDtypeStruct((DA_HEADS, n, 128), BF16),
        jax.ShapeDtypeStruct((n // ATT_TILE, DA_V_COLS, ATT_TILE), BF16),
        jax.ShapeDtypeStruct((n, RET_Q_COLS), BF16),
        jax.ShapeDtypeStruct((n, RET_K_COLS), BF16),
        jax.ShapeDtypeStruct((n, RET_WIDTH), BF16),
        jax.ShapeDtypeStruct((n, RET_WIDTH), BF16),
    )
    out_specs = (
        pl.BlockSpec((DA_Q_COLS, tm), lambda i: (0, i)),
        pl.BlockSpec((DA_HEADS, tm, 128), lambda i: (0, i, 0)),
        pl.BlockSpec((tm // ATT_TILE, DA_V_COLS, ATT_TILE), lambda i: (i, 0, 0)),
        pl.BlockSpec((tm, RET_Q_COLS), row),
        pl.BlockSpec((tm, RET_K_COLS), row),
        pl.BlockSpec((tm, RET_WIDTH), row),
        pl.BlockSpec((tm, RET_WIDTH), row),
    )
    return pl.pallas_call(
        _inproj_kernel,
        grid=(n // tm,),
        in_specs=[
            pl.BlockSpec((tm, D_MODEL), row),
            _const_spec((1, D_MODEL)),
            _const_spec(wqT.shape),
            _const_spec(wk.shape),
            _const_spec(wvT.shape),
            _const_spec(wr.shape),
        ],
        out_specs=out_specs,
        out_shape=out_shapes,
        compiler_params=pltpu.CompilerParams(
            dimension_semantics=("arbitrary",), vmem_limit_bytes=VMEM_LIMIT_BYTES),
        name="inproj",
    )(x2, g, wqT, wk, wvT, wr)


def _attn_tables():
    t = ATT_TILE
    kk = np.arange(t)[:, None]
    qq = np.arange(t)[None, :]
    allowed = (kk // CHUNK) <= (qq // CHUNK)
    diag = np.empty((DA_HEADS, t, 2 * t), np.float32)
    prow = np.zeros((DA_HEADS, 128, 2 * ATT_QTILE), np.float32)
    for h, slope in enumerate(ALIBI_SLOPES):
        c = slope * LOG2E
        d = np.where(allowed, -2.0 * c * np.maximum(kk - qq, 0), -np.inf)
        diag[h] = np.concatenate([d, d], axis=1)
        rem = c
        for j in range(POS_TERMS):
            piece = float(np.asarray(rem, np.float32).astype(jnp.bfloat16).astype(np.float32))
            prow[h, j, :] = piece * CHUNK
            prow[h, POS_TERMS + j, :] = piece
            rem = rem - piece
    return diag, prow


def _pos_features(s):
    pos = np.arange(s)
    feat = np.zeros((s, 128), np.float32)
    for j in range(POS_TERMS):
        feat[:, j] = pos // CHUNK
        feat[:, POS_TERMS + j] = pos % CHUNK
    return feat


def _attn_kernel(lam_init, qT_ref, k_ref, pos_ref, vT_ref, diag_ref, prow_ref, lamv_ref, g_ref,
                 o_ref, acc_ref, *sp_refs):
    t = ATT_TILE
    tq = ATT_QTILE
    w = 2 * tq
    hw = w // 2
    nq = qT_ref.shape[1] // tq
    s_refs = sp_refs[:ATT_LOOKAHEAD + ATT_GROUP]
    p_refs = sp_refs[ATT_LOOKAHEAD + ATT_GROUP:]
    z = jnp.zeros((DA_QK_DIM, t), BF16)
    ones_rows = (lax.broadcasted_iota(jnp.int32, (ATT_DEN_ROWS, t), 0) == 0).astype(BF16)
    diag = diag_ref[0]
    lv = lamv_ref[...]
    lam = (jnp.exp(jnp.sum(lv[0:1] * lv[1:2], axis=-1, keepdims=True))
           - jnp.exp(jnp.sum(lv[2:3] * lv[3:4], axis=-1, keepdims=True)) + lam_init)

    def keys(k0, nk):
        rows = slice(k0 * t, (k0 + nk) * t)
        return jnp.concatenate([k_ref[rows, :], pos_ref[rows, :]], axis=1)

    def values_t(k0, nk):
        return jnp.concatenate(
            [jnp.concatenate([vT_ref[ki], ones_rows], axis=0) for ki in range(k0, k0 + nk)], axis=1)

    def colmax(a):
        return jnp.max(a, axis=0, keepdims=True)

    def prob(a, m):
        return jnp.exp2(a - m).astype(BF16)

    def combine(a):
        num = a[:DA_V_DIM] * (1.0 / a[DA_V_DIM:DA_V_DIM + 1])
        return num[:, :t] - lam * num[:, t:]

    def query_operand(qi):
        cols = []
        for half in range(2):
            qh = qT_ref[:, qi * tq + half * t:qi * tq + (half + 1) * t]
            cols.append(jnp.concatenate([qh[:DA_QK_DIM], z], axis=0))
            cols.append(jnp.concatenate([z, qh[DA_QK_DIM:]], axis=0))
        return jnp.concatenate([jnp.concatenate(cols, axis=1), prow_ref[0]], axis=0)

    tasks = []
    for qi in range(nq):
        tasks.append((qi, "first", 2 * qi))
        tasks += [(qi, "full", k0) for k0 in range(2 * qi)]
        tasks.append((qi, "last", 2 * qi + 1))

    qexts = {}

    def scores(task):
        qi, kind, k0 = task
        if qi not in qexts:
            qexts[qi] = query_operand(qi)
        qext = qexts[qi]
        if kind == "last":
            s = jnp.dot(keys(k0, 1), qext[:, hw:], preferred_element_type=F32) + diag
        else:
            s = jnp.dot(keys(k0, 1), qext, preferred_element_type=F32)
            if kind == "first":
                s = jnp.concatenate([s[:, :hw] + diag, s[:, hw:]], axis=1)
        return s

    row0 = pl.multiple_of(jnp.minimum(pl.program_id(0), 0), t)

    def issue(idx):
        s = scores(tasks[idx])
        s_refs[idx % len(s_refs)][pl.ds(row0, t), :s.shape[1]] = s
        return colmax(s)

    def probabilities(gi, group, m):
        ncol = hw if tasks[group[0]][1] == "last" else w
        p_ref = p_refs[gi % len(p_refs)]
        for j, idx in enumerate(group):
            s_ref = s_refs[idx % len(s_refs)]
            for r in range(0, t, ATT_EXP_ROWS):
                rows = pl.ds(pl.multiple_of(row0 + r, ATT_EXP_ROWS), ATT_EXP_ROWS)
                out_rows = pl.ds(pl.multiple_of(row0 + j * t + r, ATT_EXP_ROWS), ATT_EXP_ROWS)
                p_ref[out_rows, :ncol] = prob(s_ref[rows, :ncol], m)
        return p_ref[pl.ds(row0, len(group) * t), :ncol]

    groups = []
    idx = 0
    for qi in range(nq):
        groups.append([idx])
        idx += 1
        for _ in range(0, 2 * qi, ATT_GROUP):
            groups.append(list(range(idx, idx + ATT_GROUP)))
            idx += ATT_GROUP
        groups.append([idx])
        idx += 1

    smaxes = []
    m = None
    for gi, group in enumerate(groups):
        while len(smaxes) < min(group[-1] + 1 + ATT_LOOKAHEAD, len(tasks)):
            smaxes.append(issue(len(smaxes)))
        qi, kind, k0 = tasks[group[0]]
        smax = functools.reduce(jnp.maximum, [smaxes[i] for i in group])
        acc = acc_ref.at[qi % 2]
        if kind == "first":
            m = smax
            acc[...] = jnp.dot(
                values_t(k0, 1), probabilities(gi, group, m), preferred_element_type=F32)
        elif kind == "full":
            m_new = jnp.maximum(m, smax)
            pv = jnp.dot(values_t(k0, len(group)), probabilities(gi, group, m_new),
                         preferred_element_type=F32)
            acc[...] = acc[...] * jnp.exp2(m - m_new) + pv
            m = m_new
        else:
            mh = m[:, hw:]
            mh_new = jnp.maximum(mh, smax)
            pv = jnp.dot(values_t(k0, 1), probabilities(gi, group, mh_new),
                         preferred_element_type=F32)
            acc_h = acc[:, hw:] * jnp.exp2(mh - mh_new) + pv
            o = jnp.concatenate([combine(acc[:, :hw]), combine(acc_h)], axis=1)
            o = o * lax.rsqrt(jnp.mean(o * o, axis=0, keepdims=True) + NORM_EPS)
            o_ref[qi * tq:(qi + 1) * tq, :] = (
                o.T * g_ref[...] * (1.0 - lam_init)).astype(o_ref.dtype)


def _attention(qT, k, vT, lamv, subln_g, batch, seq, lam_init):
    t = ATT_TILE
    tq = ATT_QTILE
    nq = seq // tq
    nk = seq // t
    n = batch * seq
    diag, prow = _attn_tables()
    diag = jnp.asarray(diag)
    prow = jnp.asarray(prow, BF16)
    pos = jnp.asarray(_pos_features(seq), BF16)
    return pl.pallas_call(
        functools.partial(_attn_kernel, lam_init),
        grid=(batch, DA_HEADS),
        in_specs=[
            pl.BlockSpec((128, seq), lambda b, h: (h, b)),
            pl.BlockSpec((None, seq, 128), lambda b, h: (h, b, 0)),
            pl.BlockSpec((seq, 128), lambda b, h: (0, 0)),
            pl.BlockSpec((nk, 128, t), lambda b, h: (b, h, 0)),
            pl.BlockSpec((1, t, 2 * t), lambda b, h: (h, 0, 0)),
            pl.BlockSpec((1, 128, 2 * tq), lambda b, h: (h, 0, 0)),
            pl.BlockSpec((4, DA_QK_DIM), lambda b, h: (0, 0)),
            pl.BlockSpec((1, DA_V_DIM), lambda b, h: (0, 0)),
        ],
        out_specs=pl.BlockSpec((seq, DA_V_DIM), lambda b, h: (b, h)),
        out_shape=jax.ShapeDtypeStruct((n, DA_WIDTH), BF16),
        scratch_shapes=[
            pltpu.VMEM((2, DA_V_DIM + ATT_DEN_ROWS, 2 * tq), F32),
            *[pltpu.VMEM((t, 2 * tq), F32) for _ in range(ATT_LOOKAHEAD + ATT_GROUP)],
            *[pltpu.VMEM((ATT_GROUP * t, 2 * tq), BF16) for _ in range(2)],
        ],
        compiler_params=pltpu.CompilerParams(
            dimension_semantics=("arbitrary", "arbitrary"),
            vmem_limit_bytes=VMEM_LIMIT_BYTES),
        name="diff_attention",
    )(qT, k, pos, vT, diag, prow, lamv, subln_g)


def _ret_tables():
    L = RET_BLOCK
    i = np.arange(L)[:, None].astype(np.float64)
    j = np.arange(L)[None, :].astype(np.float64)
    allowed = (j // CHUNK) <= (i // CHUNK)
    dmat = np.empty((RET_HEADS, L, L), np.float32)
    qdec = np.zeros((RET_HEADS, L, 128), np.float32)
    kdec = np.zeros((RET_HEADS, L, 128), np.float32)
    cdec = np.empty((RET_HEADS, 1, 128), np.float32)
    lane = np.arange(128)[None, :]
    for h, gamma in enumerate(RET_GAMMA):
        mine = (lane // RET_QK_DIM) == (h % 2)
        dmat[h] = np.where(allowed, gamma ** np.abs(i - j), 0.0) * RET_QK_DIM ** -0.5
        qdec[h] = np.where(mine, gamma ** (i + 1.0), 0.0)
        kdec[h] = np.where(mine, gamma ** (L - 1.0 - i), 0.0) * RET_QK_DIM ** -0.5
        cdec[h] = gamma ** L
    return dmat, qdec, kdec, cdec


def _ret_kernel(q_ref, k_ref, v_ref, gate_ref, dmat_ref, qdec_ref, kdec_ref, cdec_ref, g_ref,
                o_ref):
    L = RET_BLOCK
    tasks = [(blk, h) for blk in range(q_ref.shape[0] // L) for h in range(RET_HEADS)]

    def operands(task):
        blk, h = task
        rows = slice(blk * L, (blk + 1) * L)
        pair = slice((h // 2) * 128, (h // 2 + 1) * 128)
        return rows, slice(h * RET_V_DIM, (h + 1) * RET_V_DIM), q_ref[rows, pair], k_ref[rows, pair]

    def scores(task):
        _, h = task
        _, _, q, k = operands(task)
        qm = jnp.where(qdec_ref[h] > 0.0, q, jnp.zeros_like(q))
        s = lax.dot_general(qm, k, _NT, preferred_element_type=F32) * dmat_ref[h]
        return s.astype(BF16)

    states = [jnp.zeros((128, RET_V_DIM), F32) for _ in range(RET_HEADS)]
    pending = [scores(task) for task in tasks[:RET_LOOKAHEAD]]
    for idx, task in enumerate(tasks):
        _, h = task
        s = pending.pop(0)
        if idx + RET_LOOKAHEAD < len(tasks):
            pending.append(scores(tasks[idx + RET_LOOKAHEAD]))
        rows, head, q, k = operands(task)
        v = v_ref[rows, head]
        inner = jnp.dot(s, v, preferred_element_type=F32)
        qd = (q.astype(F32) * qdec_ref[h]).astype(BF16)
        cross = jnp.dot(qd, states[h].astype(BF16), preferred_element_type=F32)
        kd = (k.astype(F32) * kdec_ref[h]).astype(BF16)
        states[h] = states[h] * cdec_ref[h] + lax.dot_general(
            kd, v, _TN, preferred_element_type=F32)
        o = _rms(inner + cross) * g_ref[...]
        gate = gate_ref[rows, head].astype(F32)
        o_ref[rows, head] = (gate * (1.0 / (1.0 + jnp.exp(-gate))) * o).astype(o_ref.dtype)


def _retention(rq, rk, rv, rg, norm_g, batch, seq):
    n = batch * seq
    dmat, qdec, kdec, cdec = (jnp.asarray(a) for a in _ret_tables())
    row = lambda b: (b, 0)
    return pl.pallas_call(
        _ret_kernel,
        grid=(batch,),
        in_specs=[
            pl.BlockSpec((seq, RET_Q_COLS), row),
            pl.BlockSpec((seq, RET_K_COLS), row),
            pl.BlockSpec((seq, RET_WIDTH), row),
            pl.BlockSpec((seq, RET_WIDTH), row),
            _const_spec(dmat.shape),
            _const_spec(qdec.shape),
            _const_spec(kdec.shape),
            _const_spec(cdec.shape),
            _const_spec((1, RET_V_DIM)),
        ],
        out_specs=pl.BlockSpec((seq, RET_WIDTH), row),
        out_shape=jax.ShapeDtypeStruct((n, RET_WIDTH), BF16),
        compiler_params=pltpu.CompilerParams(
            dimension_semantics=("arbitrary",), vmem_limit_bytes=VMEM_LIMIT_BYTES),
        name="retention",
    )(rq, rk, rv, rg, dmat, qdec, kdec, cdec, norm_g)


def _ffn_kernel(x_ref, oa_ref, or_ref, wo_ref, g1_ref, g2_ref, wg_ref, wu_ref, wd_ref, g3_ref,
                out_ref):
    subs = [slice(j * FFN_SUB, (j + 1) * FFN_SUB) for j in range(x_ref.shape[0] // FFN_SUB)]
    mix = [jnp.dot(oa_ref[r, :], wo_ref[:DA_WIDTH, :], preferred_element_type=F32)
           + jnp.dot(or_ref[r, :], wo_ref[DA_WIDTH:, :], preferred_element_type=F32) for r in subs]
    x1 = [x_ref[r, :] + _rms(mx) * g1_ref[...] for r, mx in zip(subs, mix)]
    h = [(_rms(a) * g2_ref[...]).astype(BF16) for a in x1]
    gate = [jnp.dot(a, wg_ref[...], preferred_element_type=F32) for a in h]
    up = [jnp.dot(a, wu_ref[...], preferred_element_type=F32) for a in h]
    f = [(gt * (1.0 / (1.0 + jnp.exp(-gt))) * u).astype(BF16) for gt, u in zip(gate, up)]
    y = [jnp.dot(a, wd_ref[...], preferred_element_type=F32) for a in f]
    for r, a, b in zip(subs, x1, y):
        out_ref[r, :] = a + _rms(b) * g3_ref[...]


def _ffn(x2, oa, orr, wo, g1, g2, wg, wu, wd, g3):
    n = x2.shape[0]
    tm = ROW_TILE
    row = lambda i: (i, 0)
    return pl.pallas_call(
        _ffn_kernel,
        grid=(n // tm,),
        in_specs=[
            pl.BlockSpec((tm, D_MODEL), row),
            pl.BlockSpec((tm, DA_WIDTH), row),
            pl.BlockSpec((tm, RET_WIDTH), row),
            _const_spec(wo.shape),
            _const_spec((1, D_MODEL)),
            _const_spec((1, D_MODEL)),
            _const_spec(wg.shape),
            _const_spec(wu.shape),
            _const_spec(wd.shape),
            _const_spec((1, D_MODEL)),
        ],
        out_specs=pl.BlockSpec((tm, D_MODEL), row),
        out_shape=jax.ShapeDtypeStruct((n, D_MODEL), F32),
        compiler_params=pltpu.CompilerParams(
            dimension_semantics=("arbitrary",), vmem_limit_bytes=VMEM_LIMIT_BYTES),
        name="outproj_ffn",
    )(x2, oa, orr, wo, g1, g2, wg, wu, wd, g3)


def kernel(x, pre_mix_g, w_in, lambda_q1, lambda_k1, lambda_q2, lambda_k2, da_subln_g, ret_norm_g,
           w_out, post_mix_g, pre_ffn_g, w_gate, w_up, w_down, post_ffn_g):
    batch, seq, d = x.shape
    depth = w_in.shape[0]
    assert d == D_MODEL and seq % ATT_QTILE == 0 and seq % RET_BLOCK == 0
    assert ATT_QTILE == 2 * ATT_TILE and ATT_TILE % CHUNK == 0 and RET_BLOCK % CHUNK == 0
    assert (batch * seq) % ROW_TILE == 0 and ROW_TILE % FFN_SUB == 0
    assert (batch * seq) % INPROJ_TILE == 0 and INPROJ_TILE % ATT_TILE == 0
    x2 = x.reshape(batch * seq, d)
    c1 = DA_Q_COLS
    c2 = c1 + DA_K_COLS
    c3 = c2 + DA_V_COLS
    for l in range(depth):
        w = w_in[l].astype(BF16)
        qT, k, vT, rq, rk, rv, rg = _inproj(
            x2, pre_mix_g[l][None, :], w[:, :c1].T, w[:, c1:c2], w[:, c2:c3].T, w[:, c3:])
        lam_init = 0.8 - 0.6 * math.exp(-0.3 * l)
        lamv = jnp.stack([lambda_q1[l], lambda_k1[l], lambda_q2[l], lambda_k2[l]]).astype(F32)
        o_da = _attention(qT, k, vT, lamv, da_subln_g[l][None, :].astype(F32), batch, seq, lam_init)
        o_ret = _retention(rq, rk, rv, rg, ret_norm_g[l][None, :].astype(F32), batch, seq)
        x2 = _ffn(x2, o_da, o_ret, w_out[l].astype(BF16), post_mix_g[l][None, :],
                  pre_ffn_g[l][None, :], w_gate[l].astype(BF16), w_up[l].astype(BF16),
                  w_down[l].astype(BF16), post_ffn_g[l][None, :])
    return x2.reshape(batch, seq, d)
```

```python
import functools
import math

import numpy as np
import jax
import jax.numpy as jnp
from jax import lax
from jax.experimental import pallas as pl
from jax.experimental.pallas import tpu as pltpu

F32 = jnp.float32
BF16 = jnp.bfloat16

D_MODEL = 1024
CHUNK = 64
NORM_EPS = 1e-6
DA_HEADS = 4
DA_QK_DIM = 64
DA_V_DIM = 128
DA_WIDTH = DA_HEADS * DA_V_DIM
ALIBI_SLOPES = tuple(2.0 ** (-8.0 * (h + 1) / DA_HEADS) for h in range(DA_HEADS))
RET_HEADS = 4
RET_QK_DIM = 64
RET_V_DIM = 128
RET_WIDTH = RET_HEADS * RET_V_DIM
RET_GAMMA = tuple(1.0 - 2.0 ** (-5.0 - h) for h in range(RET_HEADS))
DA_Q_COLS = DA_HEADS * 2 * DA_QK_DIM
DA_K_COLS = DA_HEADS * 2 * DA_QK_DIM
DA_V_COLS = DA_WIDTH
RET_Q_COLS = RET_HEADS * RET_QK_DIM
RET_K_COLS = RET_HEADS * RET_QK_DIM

V7X_LANES = 128
V7X_BF16_SUBLANES = 16
V7X_VMEM_BYTES = 64 * 1024 * 1024
VMEM_LIMIT_BYTES = V7X_VMEM_BYTES - 8 * 1024 * 1024

INPROJ_TILE = 1024
ROW_TILE = 1024
FFN_SUB = 256
ATT_TILE = 256
ATT_QTILE = 512
ATT_DEN_ROWS = 16
ATT_GROUP = 2
ATT_LOOKAHEAD = 4
ATT_EXP_ROWS = 32
RET_BLOCK = 256
RET_LOOKAHEAD = 2
LOG2E = math.log2(math.e)
POS_TERMS = 3

_NT = (((1,), (1,)), ((), ()))
_TN = (((0,), (0,)), ((), ()))


def _rms(x):
    return x * lax.rsqrt(jnp.mean(x * x, axis=-1, keepdims=True) + NORM_EPS)


def _const_spec(shape):
    nd = len(shape)
    return pl.BlockSpec(shape, lambda *_: (0,) * nd, pipeline_mode=pl.Buffered(1))


def _inproj_kernel(x_ref, g_ref, win_ref, wo32_ref, wg32_ref, wu32_ref, wd32_ref,
                   qT_ref, k_ref, vT_ref, rq_ref, rk_ref, rv_ref, rg_ref,
                   wo_ref, wg_ref, wu_ref, wd_ref,
                   wqT_ref, wk_ref, wvT_ref, wr_ref):
    c1 = DA_Q_COLS
    c2 = c1 + DA_K_COLS
    c3 = c2 + DA_V_COLS

    @pl.when(pl.program_id(0) == 0)
    def _():
        for c in range(0, DA_Q_COLS, 128):
            wqT_ref[c:c + 128, :] = win_ref[:, c:c + 128].T.astype(BF16)
            wvT_ref[c:c + 128, :] = win_ref[:, c2 + c:c2 + c + 128].T.astype(BF16)
        wk_ref[...] = win_ref[:, c1:c2].astype(BF16)
        wr_ref[...] = win_ref[:, c3:].astype(BF16)

    wo_ref[...] = wo32_ref[...].astype(BF16)
    wg_ref[...] = wg32_ref[...].astype(BF16)
    wu_ref[...] = wu32_ref[...].astype(BF16)
    wd_ref[...] = wd32_ref[...].astype(BF16)

    for j in range(INPROJ_TILE // ATT_TILE):
        rows = slice(j * ATT_TILE, (j + 1) * ATT_TILE)
        h = (_rms(x_ref[rows, :]) * g_ref[...]).astype(BF16)
        qT = lax.dot_general(wqT_ref[...], h, _NT, preferred_element_type=F32)
        qT_ref[:, rows] = (qT * (DA_QK_DIM ** -0.5 * LOG2E)).astype(BF16)
        k = jnp.dot(h, wk_ref[...], preferred_element_type=F32)
        for hh in range(DA_HEADS):
            k_ref[hh, rows, :] = k[:, hh * 128:(hh + 1) * 128].astype(BF16)
        vT = lax.dot_general(wvT_ref[...], h, _NT, preferred_element_type=F32)
        vT_ref[j] = vT.astype(BF16)
        r = jnp.dot(h, wr_ref[...], preferred_element_type=F32)
        rq_ref[rows, :] = r[:, 0:256].astype(BF16)
        rk_ref[rows, :] = r[:, 256:512].astype(BF16)
        rv_ref[rows, :] = r[:, 512:1024].astype(BF16)
        rg_ref[rows, :] = r[:, 1024:1536].astype(BF16)


def _inproj(x2, g, w_in, w_out, w_gate, w_up, w_down):
    n = x2.shape[0]
    tm = INPROJ_TILE
    steps = n // tm
    d_ff = w_gate.shape[1]
    wo_rows, wg_rows, wd_rows = D_MODEL // steps, D_MODEL // steps, d_ff // (steps // 2)
    assert wo_rows * steps == D_MODEL and wd_rows * (steps // 2) == d_ff
    assert wo_rows % V7X_BF16_SUBLANES == 0 and wd_rows % V7X_BF16_SUBLANES == 0
    row = lambda i: (i, 0)
    out_shapes = (
        jax.ShapeDtypeStruct((DA_Q_COLS, n), BF16),
        jax.ShapeDtypeStruct((DA_HEADS, n, 128), BF16),
        jax.ShapeDtypeStruct((n // ATT_TILE, DA_V_COLS, ATT_TILE), BF16),
        jax.ShapeDtypeStruct((n, RET_Q_COLS), BF16),
        jax.ShapeDtypeStruct((n, RET_K_COLS), BF16),
        jax.ShapeDtypeStruct((n, RET_WIDTH), BF16),
        jax.ShapeDtypeStruct((n, RET_WIDTH), BF16),
        jax.ShapeDtypeStruct(w_out.shape, BF16),
        jax.ShapeDtypeStruct(w_gate.shape, BF16),
        jax.ShapeDtypeStruct(w_up.shape, BF16),
        jax.ShapeDtypeStruct(w_down.shape, BF16),
    )
    weight_specs = [
        pl.BlockSpec((wo_rows, D_MODEL), row),
        pl.BlockSpec((wg_rows, d_ff), row),
        pl.BlockSpec((wg_rows, d_ff), row),
        pl.BlockSpec((wd_rows, D_MODEL), lambda i: (i // 2, 0)),
    ]
    out_specs = (
        pl.BlockSpec((DA_Q_COLS, tm), lambda i: (0, i)),
        pl.BlockSpec((DA_HEADS, tm, 128), lambda i: (0, i, 0)),
        pl.BlockSpec((tm // ATT_TILE, DA_V_COLS, ATT_TILE), lambda i: (i, 0, 0)),
        pl.BlockSpec((tm, RET_Q_COLS), row),
        pl.BlockSpec((tm, RET_K_COLS), row),
        pl.BlockSpec((tm, RET_WIDTH), row),
        pl.BlockSpec((tm, RET_WIDTH), row),
        *weight_specs,
    )
    return pl.pallas_call(
        _inproj_kernel,
        grid=(steps,),
        in_specs=[
            pl.BlockSpec((tm, D_MODEL), row),
            _const_spec((1, D_MODEL)),
            _const_spec(w_in.shape),
            *weight_specs,
        ],
        out_specs=out_specs,
        out_shape=out_shapes,
        scratch_shapes=[
            pltpu.VMEM((DA_Q_COLS, D_MODEL), BF16),
            pltpu.VMEM((D_MODEL, DA_K_COLS), BF16),
            pltpu.VMEM((DA_V_COLS, D_MODEL), BF16),
            pltpu.VMEM((D_MODEL, w_in.shape[1] - DA_Q_COLS - DA_K_COLS - DA_V_COLS), BF16),
        ],
        compiler_params=pltpu.CompilerParams(
            dimension_semantics=("arbitrary",), vmem_limit_bytes=VMEM_LIMIT_BYTES),
        name="inproj",
    )(x2, g, w_in, w_out, w_gate, w_up, w_down)


def _attn_tables():
    t = ATT_TILE
    kk = np.arange(t)[:, None]
    qq = np.arange(t)[None, :]
    allowed = (kk // CHUNK) <= (qq // CHUNK)
    diag = np.empty((DA_HEADS, t, 2 * t), np.float32)
    prow = np.zeros((DA_HEADS, 128, 2 * ATT_QTILE), np.float32)
    for h, slope in enumerate(ALIBI_SLOPES):
        c = slope * LOG2E
        d = np.where(allowed, -2.0 * c * np.maximum(kk - qq, 0), -np.inf)
        diag[h] = np.concatenate([d, d], axis=1)
        rem = c
        for j in range(POS_TERMS):
            piece = float(np.asarray(rem, np.float32).astype(jnp.bfloat16).astype(np.float32))
            prow[h, j, :] = piece * CHUNK
            prow[h, POS_TERMS + j, :] = piece
            rem = rem - piece
    return diag, prow


def _pos_features(s):
    pos = np.arange(s)
    feat = np.zeros((s, 128), np.float32)
    for j in range(POS_TERMS):
        feat[:, j] = pos // CHUNK
        feat[:, POS_TERMS + j] = pos % CHUNK
    return feat


def _attn_kernel(lam_init, qT_ref, k_ref, pos_ref, vT_ref, diag_ref, prow_ref, lamv_ref, g_ref,
                 o_ref, acc_ref, *sp_refs):
    t = ATT_TILE
    tq = ATT_QTILE
    w = 2 * tq
    hw = w // 2
    nq = qT_ref.shape[1] // tq
    s_refs = sp_refs[:ATT_LOOKAHEAD + ATT_GROUP]
    p_refs = sp_refs[ATT_LOOKAHEAD + ATT_GROUP:]
    z = jnp.zeros((DA_QK_DIM, t), BF16)
    ones_rows = (lax.broadcasted_iota(jnp.int32, (ATT_DEN_ROWS, t), 0) == 0).astype(BF16)
    diag = diag_ref[0]
    lv = lamv_ref[...]
    lam = (jnp.exp(jnp.sum(lv[0:1] * lv[1:2], axis=-1, keepdims=True))
           - jnp.exp(jnp.sum(lv[2:3] * lv[3:4], axis=-1, keepdims=True)) + lam_init)

    def keys(k0, nk):
        rows = slice(k0 * t, (k0 + nk) * t)
        return jnp.concatenate([k_ref[rows, :], pos_ref[rows, :]], axis=1)

    def values_t(k0, nk):
        return jnp.concatenate(
            [jnp.concatenate([vT_ref[ki], ones_rows], axis=0) for ki in range(k0, k0 + nk)], axis=1)

    def colmax(a):
        return jnp.max(a, axis=0, keepdims=True)

    def prob(a, m):
        return jnp.exp2(a - m).astype(BF16)

    def combine(a):
        num = a[:DA_V_DIM] * (1.0 / a[DA_V_DIM:DA_V_DIM + 1])
        return num[:, :t] - lam * num[:, t:]

    def query_operand(qi):
        cols = []
        for half in range(2):
            qh = qT_ref[:, qi * tq + half * t:qi * tq + (half + 1) * t]
            cols.append(jnp.concatenate([qh[:DA_QK_DIM], z], axis=0))
            cols.append(jnp.concatenate([z, qh[DA_QK_DIM:]], axis=0))
        return jnp.concatenate([jnp.concatenate(cols, axis=1), prow_ref[0]], axis=0)

    tasks = []
    for qi in range(nq):
        tasks.append((qi, "first", 2 * qi))
        tasks += [(qi, "full", k0) for k0 in range(2 * qi)]
        tasks.append((qi, "last", 2 * qi + 1))

    qexts = {}

    def scores(task):
        qi, kind, k0 = task
        if qi not in qexts:
            qexts[qi] = query_operand(qi)
        qext = qexts[qi]
        if kind == "last":
            s = jnp.dot(keys(k0, 1), qext[:, hw:], preferred_element_type=F32) + diag
        else:
            s = jnp.dot(keys(k0, 1), qext, preferred_element_type=F32)
            if kind == "first":
                s = jnp.concatenate([s[:, :hw] + diag, s[:, hw:]], axis=1)
        return s

    row0 = pl.multiple_of(jnp.minimum(pl.program_id(0), 0), t)

    def issue(idx):
        s = scores(tasks[idx])
        s_refs[idx % len(s_refs)][pl.ds(row0, t), :s.shape[1]] = s
        return colmax(s)

    def probabilities(gi, group, m):
        ncol = hw if tasks[group[0]][1] == "last" else w
        p_ref = p_refs[gi % len(p_refs)]
        for j, idx in enumerate(group):
            s_ref = s_refs[idx % len(s_refs)]
            for r in range(0, t, ATT_EXP_ROWS):
                rows = pl.ds(pl.multiple_of(row0 + r, ATT_EXP_ROWS), ATT_EXP_ROWS)
                out_rows = pl.ds(pl.multiple_of(row0 + j * t + r, ATT_EXP_ROWS), ATT_EXP_ROWS)
                p_ref[out_rows, :ncol] = prob(s_ref[rows, :ncol], m)
        return p_ref[pl.ds(row0, len(group) * t), :ncol]

    groups = []
    idx = 0
    for qi in range(nq):
        groups.append([idx])
        idx += 1
        for _ in range(0, 2 * qi, ATT_GROUP):
            groups.append(list(range(idx, idx + ATT_GROUP)))
            idx += ATT_GROUP
        groups.append([idx])
        idx += 1

    smaxes = []
    m = None
    for gi, group in enumerate(groups):
        while len(smaxes) < min(group[-1] + 1 + ATT_LOOKAHEAD, len(tasks)):
            smaxes.append(issue(len(smaxes)))
        qi, kind, k0 = tasks[group[0]]
        smax = functools.reduce(jnp.maximum, [smaxes[i] for i in group])
        acc = acc_ref.at[qi % 2]
        if kind == "first":
            m = smax
            acc[...] = jnp.dot(
                values_t(k0, 1), probabilities(gi, group, m), preferred_element_type=F32)
        elif kind == "full":
            m_new = jnp.maximum(m, smax)
            pv = jnp.dot(values_t(k0, len(group)), probabilities(gi, group, m_new),
                         preferred_element_type=F32)
            acc[...] = acc[...] * jnp.exp2(m - m_new) + pv
            m = m_new
        else:
            mh = m[:, hw:]
            mh_new = jnp.maximum(mh, smax)
            pv = jnp.dot(values_t(k0, 1), probabilities(gi, group, mh_new),
                         preferred_element_type=F32)
            acc_h = acc[:, hw:] * jnp.exp2(mh - mh_new) + pv
            o = jnp.concatenate([combine(acc[:, :hw]), combine(acc_h)], axis=1)
            o = o * lax.rsqrt(jnp.mean(o * o, axis=0, keepdims=True) + NORM_EPS)
            o_ref[qi * tq:(qi + 1) * tq, :] = (
                o.T * g_ref[...] * (1.0 - lam_init)).astype(o_ref.dtype)


def _attention(qT, k, vT, lamv, subln_g, batch, seq, lam_init):
    t = ATT_TILE
    tq = ATT_QTILE
    nq = seq // tq
    nk = seq // t
    n = batch * seq
    diag, prow = _attn_tables()
    diag = jnp.asarray(diag)
    prow = jnp.asarray(prow, BF16)
    pos = jnp.asarray(_pos_features(seq), BF16)
    return pl.pallas_call(
        functools.partial(_attn_kernel, lam_init),
        grid=(batch, DA_HEADS),
        in_specs=[
            pl.BlockSpec((128, seq), lambda b, h: (h, b)),
            pl.BlockSpec((None, seq, 128), lambda b, h: (h, b, 0)),
            pl.BlockSpec((seq, 128), lambda b, h: (0, 0)),
            pl.BlockSpec((nk, 128, t), lambda b, h: (b, h, 0)),
            pl.BlockSpec((1, t, 2 * t), lambda b, h: (h, 0, 0)),
            pl.BlockSpec((1, 128, 2 * tq), lambda b, h: (h, 0, 0)),
            pl.BlockSpec((4, DA_QK_DIM), lambda b, h: (0, 0)),
            pl.BlockSpec((1, DA_V_DIM), lambda b, h: (0, 0)),
        ],
        out_specs=pl.BlockSpec((seq, DA_V_DIM), lambda b, h: (b, h)),
        out_shape=jax.ShapeDtypeStruct((n, DA_WIDTH), BF16),
        scratch_shapes=[
            pltpu.VMEM((2, DA_V_DIM + ATT_DEN_ROWS, 2 * tq), F32),
            *[pltpu.VMEM((t, 2 * tq), F32) for _ in range(ATT_LOOKAHEAD + ATT_GROUP)],
            *[pltpu.VMEM((ATT_GROUP * t, 2 * tq), BF16) for _ in range(2)],
        ],
        compiler_params=pltpu.CompilerParams(
            dimension_semantics=("arbitrary", "arbitrary"),
            vmem_limit_bytes=VMEM_LIMIT_BYTES),
        name="diff_attention",
    )(qT, k, pos, vT, diag, prow, lamv, subln_g)


def _ret_tables():
    L = RET_BLOCK
    i = np.arange(L)[:, None].astype(np.float64)
    j = np.arange(L)[None, :].astype(np.float64)
    allowed = (j // CHUNK) <= (i // CHUNK)
    dmat = np.empty((RET_HEADS, L, L), np.float32)
    qdec = np.zeros((RET_HEADS, L, 128), np.float32)
    kdec = np.zeros((RET_HEADS, L, 128), np.float32)
    cdec = np.empty((RET_HEADS, 1, 128), np.float32)
    lane = np.arange(128)[None, :]
    for h, gamma in enumerate(RET_GAMMA):
        mine = (lane // RET_QK_DIM) == (h % 2)
        dmat[h] = np.where(allowed, gamma ** np.abs(i - j), 0.0) * RET_QK_DIM ** -0.5
        qdec[h] = np.where(mine, gamma ** (i + 1.0), 0.0)
        kdec[h] = np.where(mine, gamma ** (L - 1.0 - i), 0.0) * RET_QK_DIM ** -0.5
        cdec[h] = gamma ** L
    return dmat, qdec, kdec, cdec


def _ret_kernel(q_ref, k_ref, v_ref, gate_ref, dmat_ref, qdec_ref, kdec_ref, cdec_ref, g_ref,
                o_ref):
    L = RET_BLOCK
    tasks = [(blk, h) for blk in range(q_ref.shape[0] // L) for h in range(RET_HEADS)]

    def operands(task):
        blk, h = task
        rows = slice(blk * L, (blk + 1) * L)
        pair = slice((h // 2) * 128, (h // 2 + 1) * 128)
        return rows, slice(h * RET_V_DIM, (h + 1) * RET_V_DIM), q_ref[rows, pair], k_ref[rows, pair]

    def scores(task):
        _, h = task
        _, _, q, k = operands(task)
        qm = jnp.where(qdec_ref[h] > 0.0, q, jnp.zeros_like(q))
        s = lax.dot_general(qm, k, _NT, preferred_element_type=F32) * dmat_ref[h]
        return s.astype(BF16)

    states = [jnp.zeros((128, RET_V_DIM), F32) for _ in range(RET_HEADS)]
    pending = [scores(task) for task in tasks[:RET_LOOKAHEAD]]
    for idx, task in enumerate(tasks):
        _, h = task
        s = pending.pop(0)
        if idx + RET_LOOKAHEAD < len(tasks):
            pending.append(scores(tasks[idx + RET_LOOKAHEAD]))
        rows, head, q, k = operands(task)
        v = v_ref[rows, head]
        inner = jnp.dot(s, v, preferred_element_type=F32)
        qd = (q.astype(F32) * qdec_ref[h]).astype(BF16)
        cross = jnp.dot(qd, states[h].astype(BF16), preferred_element_type=F32)
        kd = (k.astype(F32) * kdec_ref[h]).astype(BF16)
        states[h] = states[h] * cdec_ref[h] + lax.dot_general(
            kd, v, _TN, preferred_element_type=F32)
        o = _rms(inner + cross) * g_ref[...]
        gate = gate_ref[rows, head].astype(F32)
        o_ref[rows, head] = (gate * (1.0 / (1.0 + jnp.exp(-gate))) * o).astype(o_ref.dtype)


def _retention(rq, rk, rv, rg, norm_g, batch, seq):
    n = batch * seq
    dmat, qdec, kdec, cdec = (jnp.asarray(a) for a in _ret_tables())
    row = lambda b: (b, 0)
    return pl.pallas_call(
        _ret_kernel,
        grid=(batch,),
        in_specs=[
            pl.BlockSpec((seq, RET_Q_COLS), row),
            pl.BlockSpec((seq, RET_K_COLS), row),
            pl.BlockSpec((seq, RET_WIDTH), row),
            pl.BlockSpec((seq, RET_WIDTH), row),
            _const_spec(dmat.shape),
            _const_spec(qdec.shape),
            _const_spec(kdec.shape),
            _const_spec(cdec.shape),
            _const_spec((1, RET_V_DIM)),
        ],
        out_specs=pl.BlockSpec((seq, RET_WIDTH), row),
        out_shape=jax.ShapeDtypeStruct((n, RET_WIDTH), BF16),
        compiler_params=pltpu.CompilerParams(
            dimension_semantics=("arbitrary",), vmem_limit_bytes=VMEM_LIMIT_BYTES),
        name="retention",
    )(rq, rk, rv, rg, dmat, qdec, kdec, cdec, norm_g)


def _ffn_kernel(x_ref, oa_ref, or_ref, wo_ref, g1_ref, g2_ref, wg_ref, wu_ref, wd_ref, g3_ref,
                out_ref):
    subs = [slice(j * FFN_SUB, (j + 1) * FFN_SUB) for j in range(x_ref.shape[0] // FFN_SUB)]
    mix = [jnp.dot(oa_ref[r, :], wo_ref[:DA_WIDTH, :], preferred_element_type=F32)
           + jnp.dot(or_ref[r, :], wo_ref[DA_WIDTH:, :], preferred_element_type=F32) for r in subs]
    x1 = [x_ref[r, :] + _rms(mx) * g1_ref[...] for r, mx in zip(subs, mix)]
    h = [(_rms(a) * g2_ref[...]).astype(BF16) for a in x1]
    gate = [jnp.dot(a, wg_ref[...], preferred_element_type=F32) for a in h]
    up = [jnp.dot(a, wu_ref[...], preferred_element_type=F32) for a in h]
    f = [(gt * (1.0 / (1.0 + jnp.exp(-gt))) * u).astype(BF16) for gt, u in zip(gate, up)]
    y = [jnp.dot(a, wd_ref[...], preferred_element_type=F32) for a in f]
    for r, a, b in zip(subs, x1, y):
        out_ref[r, :] = a + _rms(b) * g3_ref[...]


def _ffn(x2, oa, orr, wo, g1, g2, wg, wu, wd, g3):
    n = x2.shape[0]
    tm = ROW_TILE
    row = lambda i: (i, 0)
    return pl.pallas_call(
        _ffn_kernel,
        grid=(n // tm,),
        in_specs=[
            pl.BlockSpec((tm, D_MODEL), row),
            pl.BlockSpec((tm, DA_WIDTH), row),
            pl.BlockSpec((tm, RET_WIDTH), row),
            _const_spec(wo.shape),
            _const_spec((1, D_MODEL)),
            _const_spec((1, D_MODEL)),
            _const_spec(wg.shape),
            _const_spec(wu.shape),
            _const_spec(wd.shape),
            _const_spec((1, D_MODEL)),
        ],
        out_specs=pl.BlockSpec((tm, D_MODEL), row),
        out_shape=jax.ShapeDtypeStruct((n, D_MODEL), F32),
        compiler_params=pltpu.CompilerParams(
            dimension_semantics=("arbitrary",), vmem_limit_bytes=VMEM_LIMIT_BYTES),
        name="outproj_ffn",
    )(x2, oa, orr, wo, g1, g2, wg, wu, wd, g3)


def kernel(x, pre_mix_g, w_in, lambda_q1, lambda_k1, lambda_q2, lambda_k2, da_subln_g, ret_norm_g,
           w_out, post_mix_g, pre_ffn_g, w_gate, w_up, w_down, post_ffn_g):
    batch, seq, d = x.shape
    depth = w_in.shape[0]
    assert d == D_MODEL and seq % ATT_QTILE == 0 and seq % RET_BLOCK == 0
    assert ATT_QTILE == 2 * ATT_TILE and ATT_TILE % CHUNK == 0 and RET_BLOCK % CHUNK == 0
    assert (batch * seq) % ROW_TILE == 0 and ROW_TILE % FFN_SUB == 0
    assert (batch * seq) % INPROJ_TILE == 0 and INPROJ_TILE % ATT_TILE == 0
    x2 = x.reshape(batch * seq, d)
    for l in range(depth):
        qT, k, vT, rq, rk, rv, rg, wo, wg, wu, wd = _inproj(
            x2, pre_mix_g[l][None, :], w_in[l], w_out[l], w_gate[l], w_up[l], w_down[l])
        lam_init = 0.8 - 0.6 * math.exp(-0.3 * l)
        lamv = jnp.stack([lambda_q1[l], lambda_k1[l], lambda_q2[l], lambda_k2[l]]).astype(F32)
        o_da = _attention(qT, k, vT, lamv, da_subln_g[l][None, :].astype(F32), batch, seq, lam_init)
        o_ret = _retention(rq, rk, rv, rg, ret_norm_g[l][None, :].astype(F32), batch, seq)
        x2 = _ffn(x2, o_da, o_ret, wo, post_mix_g[l][None, :], pre_ffn_g[l][None, :],
                  wg, wu, wd, post_ffn_g[l][None, :])
    return x2.reshape(batch, seq, d)
```

```python
import functools
import math

import numpy as np
import jax
import jax.numpy as jnp
from jax import lax
from jax.experimental import pallas as pl
from jax.experimental.pallas import tpu as pltpu

F32 = jnp.float32
BF16 = jnp.bfloat16

D_MODEL = 1024
CHUNK = 64
NORM_EPS = 1e-6
DA_HEADS = 4
DA_QK_DIM = 64
DA_V_DIM = 128
DA_WIDTH = DA_HEADS * DA_V_DIM
ALIBI_SLOPES = tuple(2.0 ** (-8.0 * (h + 1) / DA_HEADS) for h in range(DA_HEADS))
RET_HEADS = 4
RET_QK_DIM = 64
RET_V_DIM = 128
RET_WIDTH = RET_HEADS * RET_V_DIM
RET_GAMMA = tuple(1.0 - 2.0 ** (-5.0 - h) for h in range(RET_HEADS))
DA_Q_COLS = DA_HEADS * 2 * DA_QK_DIM
DA_K_COLS = DA_HEADS * 2 * DA_QK_DIM
DA_V_COLS = DA_WIDTH
RET_Q_COLS = RET_HEADS * RET_QK_DIM
RET_K_COLS = RET_HEADS * RET_QK_DIM

V7X_LANES = 128
V7X_BF16_SUBLANES = 16
V7X_VMEM_BYTES = 64 * 1024 * 1024
VMEM_LIMIT_BYTES = V7X_VMEM_BYTES - 8 * 1024 * 1024

INPROJ_TILE = 1024
ROW_TILE = 1024
FFN_SUB = 256
ATT_TILE = 256
ATT_QTILE = 512
ATT_DEN_ROWS = 16
ATT_GROUP = 2
ATT_LOOKAHEAD = 4
ATT_EXP_ROWS = 32
RET_BLOCK = 256
RET_LOOKAHEAD = 2
LOG2E = math.log2(math.e)
POS_TERMS = 3

_NT = (((1,), (1,)), ((), ()))
_TN = (((0,), (0,)), ((), ()))


def _rms(x):
    return x * lax.rsqrt(jnp.mean(x * x, axis=-1, keepdims=True) + NORM_EPS)


def _silu(x):
    hx = 0.5 * x
    return hx * (jnp.tanh(hx) + 1.0)


def _const_spec(shape):
    nd = len(shape)
    return pl.BlockSpec(shape, lambda *_: (0,) * nd, pipeline_mode=pl.Buffered(1))


def _inproj_kernel(x_ref, g_ref, win_ref, wo32_ref, wg32_ref, wu32_ref, wd32_ref,
                   qT_ref, k_ref, vT_ref, rq_ref, rk_ref, rv_ref, rg_ref,
                   wo_ref, wg_ref, wu_ref, wd_ref,
                   wqT_ref, wk_ref, wvT_ref, wr_ref):
    c1 = DA_Q_COLS
    c2 = c1 + DA_K_COLS
    c3 = c2 + DA_V_COLS

    @pl.when(pl.program_id(0) == 0)
    def _():
        for c in range(0, DA_Q_COLS, 128):
            wqT_ref[c:c + 128, :] = win_ref[:, c:c + 128].T.astype(BF16)
            wvT_ref[c:c + 128, :] = win_ref[:, c2 + c:c2 + c + 128].T.astype(BF16)
        wk_ref[...] = win_ref[:, c1:c2].astype(BF16)
        wr_ref[...] = win_ref[:, c3:].astype(BF16)

    wo_ref[...] = wo32_ref[...].astype(BF16)
    wg_ref[...] = wg32_ref[...].astype(BF16)
    wu_ref[...] = wu32_ref[...].astype(BF16)
    wd_ref[...] = wd32_ref[...].astype(BF16)

    for j in range(INPROJ_TILE // ATT_TILE):
        rows = slice(j * ATT_TILE, (j + 1) * ATT_TILE)
        h = (_rms(x_ref[rows, :]) * g_ref[...]).astype(BF16)
        qT = lax.dot_general(wqT_ref[...], h, _NT, preferred_element_type=F32)
        qT_ref[:, rows] = (qT * (DA_QK_DIM ** -0.5 * LOG2E)).astype(BF16)
        k = jnp.dot(h, wk_ref[...], preferred_element_type=F32)
        for hh in range(DA_HEADS):
            k_ref[hh, rows, :] = k[:, hh * 128:(hh + 1) * 128].astype(BF16)
        vT = lax.dot_general(wvT_ref[...], h, _NT, preferred_element_type=F32)
        vT_ref[j] = vT.astype(BF16)
        r = jnp.dot(h, wr_ref[...], preferred_element_type=F32)
        rq_ref[rows, :] = r[:, 0:256].astype(BF16)
        rk_ref[rows, :] = r[:, 256:512].astype(BF16)
        rv_ref[rows, :] = r[:, 512:1024].astype(BF16)
        rg_ref[rows, :] = r[:, 1024:1536].astype(BF16)


def _inproj(x2, g, w_in, w_out, w_gate, w_up, w_down):
    n = x2.shape[0]
    tm = INPROJ_TILE
    steps = n // tm
    d_ff = w_gate.shape[1]
    wo_rows, wg_rows, wd_rows = D_MODEL // steps, D_MODEL // steps, d_ff // (steps // 2)
    assert wo_rows * steps == D_MODEL and wd_rows * (steps // 2) == d_ff
    assert wo_rows % V7X_BF16_SUBLANES == 0 and wd_rows % V7X_BF16_SUBLANES == 0
    row = lambda i: (i, 0)
    out_shapes = (
        jax.ShapeDtypeStruct((DA_Q_COLS, n), BF16),
        jax.ShapeDtypeStruct((DA_HEADS, n, 128), BF16),
        jax.ShapeDtypeStruct((n // ATT_TILE, DA_V_COLS, ATT_TILE), BF16),
        jax.ShapeDtypeStruct((n, RET_Q_COLS), BF16),
        jax.ShapeDtypeStruct((n, RET_K_COLS), BF16),
        jax.ShapeDtypeStruct((n, RET_WIDTH), BF16),
        jax.ShapeDtypeStruct((n, RET_WIDTH), BF16),
        jax.ShapeDtypeStruct(w_out.shape, BF16),
        jax.ShapeDtypeStruct(w_gate.shape, BF16),
        jax.ShapeDtypeStruct(w_up.shape, BF16),
        jax.ShapeDtypeStruct(w_down.shape, BF16),
    )
    weight_specs = [
        pl.BlockSpec((wo_rows, D_MODEL), row),
        pl.BlockSpec((wg_rows, d_ff), row),
        pl.BlockSpec((wg_rows, d_ff), row),
        pl.BlockSpec((wd_rows, D_MODEL), lambda i: (i // 2, 0)),
    ]
    out_specs = (
        pl.BlockSpec((DA_Q_COLS, tm), lambda i: (0, i)),
        pl.BlockSpec((DA_HEADS, tm, 128), lambda i: (0, i, 0)),
        pl.BlockSpec((tm // ATT_TILE, DA_V_COLS, ATT_TILE), lambda i: (i, 0, 0)),
        pl.BlockSpec((tm, RET_Q_COLS), row),
        pl.BlockSpec((tm, RET_K_COLS), row),
        pl.BlockSpec((tm, RET_WIDTH), row),
        pl.BlockSpec((tm, RET_WIDTH), row),
        *weight_specs,
    )
    return pl.pallas_call(
        _inproj_kernel,
        grid=(steps,),
        in_specs=[
            pl.BlockSpec((tm, D_MODEL), row),
            _const_spec((1, D_MODEL)),
            _const_spec(w_in.shape),
            *weight_specs,
        ],
        out_specs=out_specs,
        out_shape=out_shapes,
        scratch_shapes=[
            pltpu.VMEM((DA_Q_COLS, D_MODEL), BF16),
            pltpu.VMEM((D_MODEL, DA_K_COLS), BF16),
            pltpu.VMEM((DA_V_COLS, D_MODEL), BF16),
            pltpu.VMEM((D_MODEL, w_in.shape[1] - DA_Q_COLS - DA_K_COLS - DA_V_COLS), BF16),
        ],
        compiler_params=pltpu.CompilerParams(
            dimension_semantics=("arbitrary",), vmem_limit_bytes=VMEM_LIMIT_BYTES),
        name="inproj",
    )(x2, g, w_in, w_out, w_gate, w_up, w_down)


def _attn_tables():
    t = ATT_TILE
    kk = np.arange(t)[:, None]
    qq = np.arange(t)[None, :]
    allowed = (kk // CHUNK) <= (qq // CHUNK)
    diag = np.empty((DA_HEADS, t, 2 * t), np.float32)
    prow = np.zeros((DA_HEADS, 128, 2 * ATT_QTILE), np.float32)
    for h, slope in enumerate(ALIBI_SLOPES):
        c = slope * LOG2E
        d = np.where(allowed, -2.0 * c * np.maximum(kk - qq, 0), -np.inf)
        diag[h] = np.concatenate([d, d], axis=1)
        rem = c
        for j in range(POS_TERMS):
            piece = float(np.asarray(rem, np.float32).astype(jnp.bfloat16).astype(np.float32))
            prow[h, j, :] = piece * CHUNK
            prow[h, POS_TERMS + j, :] = piece
            rem = rem - piece
    return diag, prow


def _pos_features(s):
    pos = np.arange(s)
    feat = np.zeros((s, 128), np.float32)
    for j in range(POS_TERMS):
        feat[:, j] = pos // CHUNK
        feat[:, POS_TERMS + j] = pos % CHUNK
    return feat


def _attn_kernel(lam_init, qT_ref, k_ref, pos_ref, vT_ref, diag_ref, prow_ref, lamv_ref, g_ref,
                 o_ref, acc_ref, *sp_refs):
    t = ATT_TILE
    tq = ATT_QTILE
    w = 2 * tq
    hw = w // 2
    nq = qT_ref.shape[1] // tq
    s_refs = sp_refs[:ATT_LOOKAHEAD + ATT_GROUP]
    p_refs = sp_refs[ATT_LOOKAHEAD + ATT_GROUP:]
    z = jnp.zeros((DA_QK_DIM, t), BF16)
    ones_rows = (lax.broadcasted_iota(jnp.int32, (ATT_DEN_ROWS, t), 0) == 0).astype(BF16)
    diag = diag_ref[0]
    lv = lamv_ref[...]
    lam = (jnp.exp(jnp.sum(lv[0:1] * lv[1:2], axis=-1, keepdims=True))
           - jnp.exp(jnp.sum(lv[2:3] * lv[3:4], axis=-1, keepdims=True)) + lam_init)

    def keys(k0, nk):
        rows = slice(k0 * t, (k0 + nk) * t)
        return jnp.concatenate([k_ref[rows, :], pos_ref[rows, :]], axis=1)

    def values_t(k0, nk):
        return jnp.concatenate(
            [jnp.concatenate([vT_ref[ki], ones_rows], axis=0) for ki in range(k0, k0 + nk)], axis=1)

    def colmax(a):
        return jnp.max(a, axis=0, keepdims=True)

    def prob(a, m):
        return jnp.exp2(a - m).astype(BF16)

    def combine(a):
        num = a[:DA_V_DIM] * (1.0 / a[DA_V_DIM:DA_V_DIM + 1])
        return num[:, :t] - lam * num[:, t:]

    def query_operand(qi):
        cols = []
        for half in range(2):
            qh = qT_ref[:, qi * tq + half * t:qi * tq + (half + 1) * t]
            cols.append(jnp.concatenate([qh[:DA_QK_DIM], z], axis=0))
            cols.append(jnp.concatenate([z, qh[DA_QK_DIM:]], axis=0))
        return jnp.concatenate([jnp.concatenate(cols, axis=1), prow_ref[0]], axis=0)

    tasks = []
    for qi in range(nq):
        tasks.append((qi, "first", 2 * qi))
        tasks += [(qi, "full", k0) for k0 in range(2 * qi)]
        tasks.append((qi, "last", 2 * qi + 1))

    qexts = {}

    def scores(task):
        qi, kind, k0 = task
        if qi not in qexts:
            qexts[qi] = query_operand(qi)
        qext = qexts[qi]
        if kind == "last":
            s = jnp.dot(keys(k0, 1), qext[:, hw:], preferred_element_type=F32) + diag
        else:
            s = jnp.dot(keys(k0, 1), qext, preferred_element_type=F32)
            if kind == "first":
                s = jnp.concatenate([s[:, :hw] + diag, s[:, hw:]], axis=1)
        return s

    row0 = pl.multiple_of(jnp.minimum(pl.program_id(0), 0), t)

    def issue(idx):
        s = scores(tasks[idx])
        s_refs[idx % len(s_refs)][pl.ds(row0, t), :s.shape[1]] = s
        return colmax(s)

    def probabilities(gi, group, m):
        ncol = hw if tasks[group[0]][1] == "last" else w
        p_ref = p_refs[gi % len(p_refs)]
        for j, idx in enumerate(group):
            s_ref = s_refs[idx % len(s_refs)]
            for r in range(0, t, ATT_EXP_ROWS):
                rows = pl.ds(pl.multiple_of(row0 + r, ATT_EXP_ROWS), ATT_EXP_ROWS)
                out_rows = pl.ds(pl.multiple_of(row0 + j * t + r, ATT_EXP_ROWS), ATT_EXP_ROWS)
                p_ref[out_rows, :ncol] = prob(s_ref[rows, :ncol], m)
        return p_ref[pl.ds(row0, len(group) * t), :ncol]

    groups = []
    idx = 0
    for qi in range(nq):
        groups.append([idx])
        idx += 1
        for _ in range(0, 2 * qi, ATT_GROUP):
            groups.append(list(range(idx, idx + ATT_GROUP)))
            idx += ATT_GROUP
        groups.append([idx])
        idx += 1

    smaxes = []
    m = None
    for gi, group in enumerate(groups):
        while len(smaxes) < min(group[-1] + 1 + ATT_LOOKAHEAD, len(tasks)):
            smaxes.append(issue(len(smaxes)))
        qi, kind, k0 = tasks[group[0]]
        smax = functools.reduce(jnp.maximum, [smaxes[i] for i in group])
        acc = acc_ref.at[qi % 2]
        if kind == "first":
            m = smax
            acc[...] = jnp.dot(
                values_t(k0, 1), probabilities(gi, group, m), preferred_element_type=F32)
        elif kind == "full":
            m_new = jnp.maximum(m, smax)
            pv = jnp.dot(values_t(k0, len(group)), probabilities(gi, group, m_new),
                         preferred_element_type=F32)
            acc[...] = acc[...] * jnp.exp2(m - m_new) + pv
            m = m_new
        else:
            mh = m[:, hw:]
            mh_new = jnp.maximum(mh, smax)
            pv = jnp.dot(values_t(k0, 1), probabilities(gi, group, mh_new),
                         preferred_element_type=F32)
            acc_h = acc[:, hw:] * jnp.exp2(mh - mh_new) + pv
            o = jnp.concatenate([combine(acc[:, :hw]), combine(acc_h)], axis=1)
            o = o * lax.rsqrt(jnp.mean(o * o, axis=0, keepdims=True) + NORM_EPS)
            o_ref[qi * tq:(qi + 1) * tq, :] = (
                o.T * g_ref[...] * (1.0 - lam_init)).astype(o_ref.dtype)


def _attention(qT, k, vT, lamv, subln_g, batch, seq, lam_init):
    t = ATT_TILE
    tq = ATT_QTILE
    nk = seq // t
    n = batch * seq
    diag, prow = _attn_tables()
    diag = jnp.asarray(diag)
    prow = jnp.asarray(prow, BF16)
    pos = jnp.asarray(_pos_features(seq), BF16)
    return pl.pallas_call(
        functools.partial(_attn_kernel, lam_init),
        grid=(batch, DA_HEADS),
        in_specs=[
            pl.BlockSpec((128, seq), lambda b, h: (h, b)),
            pl.BlockSpec((None, seq, 128), lambda b, h: (h, b, 0)),
            pl.BlockSpec((seq, 128), lambda b, h: (0, 0)),
            pl.BlockSpec((nk, 128, t), lambda b, h: (b, h, 0)),
            pl.BlockSpec((1, t, 2 * t), lambda b, h: (h, 0, 0)),
            pl.BlockSpec((1, 128, 2 * tq), lambda b, h: (h, 0, 0)),
            pl.BlockSpec((4, DA_QK_DIM), lambda b, h: (0, 0)),
            pl.BlockSpec((1, DA_V_DIM), lambda b, h: (0, 0)),
        ],
        out_specs=pl.BlockSpec((seq, DA_V_DIM), lambda b, h: (b, h)),
        out_shape=jax.ShapeDtypeStruct((n, DA_WIDTH), BF16),
        scratch_shapes=[
            pltpu.VMEM((2, DA_V_DIM + ATT_DEN_ROWS, 2 * tq), F32),
            *[pltpu.VMEM((t, 2 * tq), F32) for _ in range(ATT_LOOKAHEAD + ATT_GROUP)],
            *[pltpu.VMEM((ATT_GROUP * t, 2 * tq), BF16) for _ in range(2)],
        ],
        compiler_params=pltpu.CompilerParams(
            dimension_semantics=("arbitrary", "arbitrary"),
            vmem_limit_bytes=VMEM_LIMIT_BYTES),
        name="diff_attention",
    )(qT, k, pos, vT, diag, prow, lamv, subln_g)


def _ret_tables():
    L = RET_BLOCK
    i = np.arange(L)[:, None].astype(np.float64)
    j = np.arange(L)[None, :].astype(np.float64)
    allowed = (j // CHUNK) <= (i // CHUNK)
    dmat = np.empty((RET_HEADS, L, L), np.float32)
    lanes = np.zeros((RET_HEADS, 1, 128), np.float32)
    qdec = np.empty((RET_HEADS, L, RET_V_DIM), np.float32)
    kdec = np.zeros((RET_HEADS, L, 128), np.float32)
    cdec = np.empty((RET_HEADS, 1, 128), np.float32)
    lane = np.arange(128)[None, :]
    for h, gamma in enumerate(RET_GAMMA):
        mine = (lane // RET_QK_DIM) == (h % 2)
        dmat[h] = np.where(allowed, gamma ** np.abs(i - j), 0.0) * RET_QK_DIM ** -0.5
        lanes[h] = mine
        qdec[h] = gamma ** (i + 1.0)
        kdec[h] = np.where(mine, gamma ** (L - 1.0 - i), 0.0) * RET_QK_DIM ** -0.5
        cdec[h] = gamma ** L
    return dmat, lanes, qdec, kdec, cdec


def _ret_kernel(q_ref, k_ref, v_ref, gate_ref, dmat_ref, lanes_ref, qdec_ref, kdec_ref, cdec_ref,
                g_ref, o_ref):
    L = RET_BLOCK
    tasks = [(blk, h) for blk in range(q_ref.shape[0] // L) for h in range(RET_HEADS)]
    own_lanes = [jnp.broadcast_to(lanes_ref[h] > 0.0, (L, 128)) for h in range(RET_HEADS)]

    def operands(task):
        blk, h = task
        rows = slice(blk * L, (blk + 1) * L)
        pair = slice((h // 2) * 128, (h // 2 + 1) * 128)
        return rows, slice(h * RET_V_DIM, (h + 1) * RET_V_DIM), q_ref[rows, pair], k_ref[rows, pair]

    def scores(task):
        _, h = task
        _, _, q, k = operands(task)
        qm = jnp.where(own_lanes[h], q, jnp.zeros_like(q))
        s = lax.dot_general(qm, k, _NT, preferred_element_type=F32) * dmat_ref[h]
        return s.astype(BF16)

    states = [jnp.zeros((128, RET_V_DIM), F32) for _ in range(RET_HEADS)]
    pending = [scores(task) for task in tasks[:RET_LOOKAHEAD]]
    for idx, task in enumerate(tasks):
        _, h = task
        s = pending.pop(0)
        if idx + RET_LOOKAHEAD < len(tasks):
            pending.append(scores(tasks[idx + RET_LOOKAHEAD]))
        rows, head, q, k = operands(task)
        v = v_ref[rows, head]
        inner = jnp.dot(s, v, preferred_element_type=F32)
        cross = jnp.dot(q, states[h].astype(BF16), preferred_element_type=F32) * qdec_ref[h]
        kd = (k.astype(F32) * kdec_ref[h]).astype(BF16)
        states[h] = states[h] * cdec_ref[h] + lax.dot_general(
            kd, v, _TN, preferred_element_type=F32)
        o = _rms(inner + cross) * g_ref[...]
        o_ref[rows, head] = (_silu(gate_ref[rows, head].astype(F32)) * o).astype(o_ref.dtype)


def _retention(rq, rk, rv, rg, norm_g, batch, seq):
    n = batch * seq
    tables = [jnp.asarray(a) for a in _ret_tables()]
    row = lambda b: (b, 0)
    return pl.pallas_call(
        _ret_kernel,
        grid=(batch,),
        in_specs=[
            pl.BlockSpec((seq, RET_Q_COLS), row),
            pl.BlockSpec((seq, RET_K_COLS), row),
            pl.BlockSpec((seq, RET_WIDTH), row),
            pl.BlockSpec((seq, RET_WIDTH), row),
            *[_const_spec(a.shape) for a in tables],
            _const_spec((1, RET_V_DIM)),
        ],
        out_specs=pl.BlockSpec((seq, RET_WIDTH), row),
        out_shape=jax.ShapeDtypeStruct((n, RET_WIDTH), BF16),
        compiler_params=pltpu.CompilerParams(
            dimension_semantics=("arbitrary",), vmem_limit_bytes=VMEM_LIMIT_BYTES),
        name="retention",
    )(rq, rk, rv, rg, *tables, norm_g)


def _ffn_kernel(x_ref, oa_ref, or_ref, wo_ref, g1_ref, g2_ref, wg_ref, wu_ref, wd_ref, g3_ref,
                out_ref):
    subs = [slice(j * FFN_SUB, (j + 1) * FFN_SUB) for j in range(x_ref.shape[0] // FFN_SUB)]
    mix = [jnp.dot(oa_ref[r, :], wo_ref[:DA_WIDTH, :], preferred_element_type=F32)
           + jnp.dot(or_ref[r, :], wo_ref[DA_WIDTH:, :], preferred_element_type=F32) for r in subs]
    x1 = [x_ref[r, :] + _rms(mx) * g1_ref[...] for r, mx in zip(subs, mix)]
    h = [(_rms(a) * g2_ref[...]).astype(BF16) for a in x1]
    gate = [jnp.dot(a, wg_ref[...], preferred_element_type=F32) for a in h]
    up = [jnp.dot(a, wu_ref[...], preferred_element_type=F32) for a in h]
    f = [(_silu(gt) * u).astype(BF16) for gt, u in zip(gate, up)]
    y = [jnp.dot(a, wd_ref[...], preferred_element_type=F32) for a in f]
    for r, a, b in zip(subs, x1, y):
        out_ref[r, :] = a + _rms(b) * g3_ref[...]


def _ffn(x2, oa, orr, wo, g1, g2, wg, wu, wd, g3):
    n = x2.shape[0]
    tm = ROW_TILE
    row = lambda i: (i, 0)
    return pl.pallas_call(
        _ffn_kernel,
        grid=(n // tm,),
        in_specs=[
            pl.BlockSpec((tm, D_MODEL), row),
            pl.BlockSpec((tm, DA_WIDTH), row),
            pl.BlockSpec((tm, RET_WIDTH), row),
            _const_spec(wo.shape),
            _const_spec((1, D_MODEL)),
            _const_spec((1, D_MODEL)),
            _const_spec(wg.shape),
            _const_spec(wu.shape),
            _const_spec(wd.shape),
            _const_spec((1, D_MODEL)),
        ],
        out_specs=pl.BlockSpec((tm, D_MODEL), row),
        out_shape=jax.ShapeDtypeStruct((n, D_MODEL), F32),
        compiler_params=pltpu.CompilerParams(
            dimension_semantics=("arbitrary",), vmem_limit_bytes=VMEM_LIMIT_BYTES),
        name="outproj_ffn",
    )(x2, oa, orr, wo, g1, g2, wg, wu, wd, g3)


def kernel(x, pre_mix_g, w_in, lambda_q1, lambda_k1, lambda_q2, lambda_k2, da_subln_g, ret_norm_g,
           w_out, post_mix_g, pre_ffn_g, w_gate, w_up, w_down, post_ffn_g):
    batch, seq, d = x.shape
    depth = w_in.shape[0]
    assert d == D_MODEL and seq % ATT_QTILE == 0 and seq % RET_BLOCK == 0
    assert ATT_QTILE == 2 * ATT_TILE and ATT_TILE % CHUNK == 0 and RET_BLOCK % CHUNK == 0
    assert (batch * seq) % ROW_TILE == 0 and ROW_TILE % FFN_SUB == 0
    assert (batch * seq) % INPROJ_TILE == 0 and INPROJ_TILE % ATT_TILE == 0
    x2 = x.reshape(batch * seq, d)
    for l in range(depth):
        qT, k, vT, rq, rk, rv, rg, wo, wg, wu, wd = _inproj(
            x2, pre_mix_g[l][None, :], w_in[l], w_out[l], w_gate[l], w_up[l], w_down[l])
        lam_init = 0.8 - 0.6 * math.exp(-0.3 * l)
        lamv = jnp.stack([lambda_q1[l], lambda_k1[l], lambda_q2[l], lambda_k2[l]]).astype(F32)
        o_da = _attention(qT, k, vT, lamv, da_subln_g[l][None, :].astype(F32), batch, seq, lam_init)
        o_ret = _retention(rq, rk, rv, rg, ret_norm_g[l][None, :].astype(F32), batch, seq)
        x2 = _ffn(x2, o_da, o_ret, wo, post_mix_g[l][None, :], pre_ffn_g[l][None, :],
                  wg, wu, wd, post_ffn_g[l][None, :])
    return x2.reshape(batch, seq, d)
```

```python
import functools
import math

import numpy as np
import jax
import jax.numpy as jnp
from jax import lax
from jax.experimental import pallas as pl
from jax.experimental.pallas import tpu as pltpu

F32 = jnp.float32
BF16 = jnp.bfloat16

D_MODEL = 1024
CHUNK = 64
NORM_EPS = 1e-6
DA_HEADS = 4
DA_QK_DIM = 64
DA_V_DIM = 128
DA_WIDTH = DA_HEADS * DA_V_DIM
ALIBI_SLOPES = tuple(2.0 ** (-8.0 * (h + 1) / DA_HEADS) for h in range(DA_HEADS))
RET_HEADS = 4
RET_QK_DIM = 64
RET_V_DIM = 128
RET_WIDTH = RET_HEADS * RET_V_DIM
RET_GAMMA = tuple(1.0 - 2.0 ** (-5.0 - h) for h in range(RET_HEADS))
DA_Q_COLS = DA_HEADS * 2 * DA_QK_DIM
DA_K_COLS = DA_HEADS * 2 * DA_QK_DIM
DA_V_COLS = DA_WIDTH
RET_Q_COLS = RET_HEADS * RET_QK_DIM
RET_K_COLS = RET_HEADS * RET_QK_DIM
DA_QK_PAIR = 2 * DA_QK_DIM
RET_QK_PAIR = 2 * RET_QK_DIM

V7X_LANES = 128
V7X_BF16_SUBLANES = 16
V7X_MXU_DIM = 256
V7X_VMEM_BYTES = 64 * 1024 * 1024
VMEM_LIMIT_BYTES = V7X_VMEM_BYTES - 8 * 1024 * 1024

INPROJ_TILE = 1024
ROW_TILE = 1024
FFN_SUB = 256
ATT_TILE = 256
ATT_QTILE = 512
ATT_DEN_ROWS = 16
ATT_GROUP = 2
ATT_LOOKAHEAD = 4
ATT_EXP_ROWS = 32
ATT_POS_ROWS = V7X_MXU_DIM - DA_QK_PAIR
RET_BLOCK = 256
RET_LOOKAHEAD = 2
LOG2E = math.log2(math.e)
POS_TERMS = 3

_NT = (((1,), (1,)), ((), ()))
_TN = (((0,), (0,)), ((), ()))


def _rms(x):
    return x * lax.rsqrt(jnp.mean(x * x, axis=-1, keepdims=True) + NORM_EPS)


def _silu(x):
    hx = 0.5 * x
    return hx * (jnp.tanh(hx) + 1.0)


def _const_spec(shape):
    nd = len(shape)
    return pl.BlockSpec(shape, lambda *_: (0,) * nd, pipeline_mode=pl.Buffered(1))


def _inproj_kernel(x_ref, g_ref, win_ref, wo32_ref, wg32_ref, wu32_ref, wd32_ref,
                   qT_ref, k_ref, vT_ref, rq_ref, rk_ref, rv_ref, rg_ref,
                   wo_ref, wg_ref, wu_ref, wd_ref,
                   wqT_ref, wk_ref, wvT_ref, wr_ref):
    c1 = DA_Q_COLS
    c2 = c1 + DA_K_COLS
    c3 = c2 + DA_V_COLS

    @pl.when(pl.program_id(0) == 0)
    def _():
        for c in range(0, DA_Q_COLS, V7X_LANES):
            cols = slice(c, c + V7X_LANES)
            wqT_ref[cols, :] = win_ref[:, cols].T.astype(BF16)
            wvT_ref[cols, :] = win_ref[:, c2 + c:c2 + c + V7X_LANES].T.astype(BF16)
        wk_ref[...] = win_ref[:, c1:c2].astype(BF16)
        wr_ref[...] = win_ref[:, c3:].astype(BF16)

    wo_ref[...] = wo32_ref[...].astype(BF16)
    wg_ref[...] = wg32_ref[...].astype(BF16)
    wu_ref[...] = wu32_ref[...].astype(BF16)
    wd_ref[...] = wd32_ref[...].astype(BF16)

    for j in range(INPROJ_TILE // ATT_TILE):
        rows = slice(j * ATT_TILE, (j + 1) * ATT_TILE)
        h = (_rms(x_ref[rows, :]) * g_ref[...]).astype(BF16)
        qT = lax.dot_general(wqT_ref[...], h, _NT, preferred_element_type=F32)
        qT_ref[:, rows] = (qT * (DA_QK_DIM ** -0.5 * LOG2E)).astype(BF16)
        k = jnp.dot(h, wk_ref[...], preferred_element_type=F32)
        for hh in range(DA_HEADS):
            k_ref[hh, rows, :] = k[:, hh * DA_QK_PAIR:(hh + 1) * DA_QK_PAIR].astype(BF16)
        vT = lax.dot_general(wvT_ref[...], h, _NT, preferred_element_type=F32)
        vT_ref[j] = vT.astype(BF16)
        r = jnp.dot(h, wr_ref[...], preferred_element_type=F32)
        r1 = RET_Q_COLS
        r2 = r1 + RET_K_COLS
        r3 = r2 + RET_WIDTH
        rq_ref[rows, :] = r[:, :r1].astype(BF16)
        rk_ref[rows, :] = r[:, r1:r2].astype(BF16)
        rv_ref[rows, :] = r[:, r2:r3].astype(BF16)
        rg_ref[rows, :] = r[:, r3:].astype(BF16)


def _inproj(x2, g, w_in, w_out, w_gate, w_up, w_down):
    n = x2.shape[0]
    tm = INPROJ_TILE
    steps = n // tm
    d_ff = w_gate.shape[1]
    wo_rows, wg_rows, wd_rows = D_MODEL // steps, D_MODEL // steps, d_ff // (steps // 2)
    assert wo_rows * steps == D_MODEL and wd_rows * (steps // 2) == d_ff
    assert wo_rows % V7X_BF16_SUBLANES == 0 and wd_rows % V7X_BF16_SUBLANES == 0
    row = lambda i: (i, 0)
    out_shapes = (
        jax.ShapeDtypeStruct((DA_Q_COLS, n), BF16),
        jax.ShapeDtypeStruct((DA_HEADS, n, DA_QK_PAIR), BF16),
        jax.ShapeDtypeStruct((n // ATT_TILE, DA_V_COLS, ATT_TILE), BF16),
        jax.ShapeDtypeStruct((n, RET_Q_COLS), BF16),
        jax.ShapeDtypeStruct((n, RET_K_COLS), BF16),
        jax.ShapeDtypeStruct((n, RET_WIDTH), BF16),
        jax.ShapeDtypeStruct((n, RET_WIDTH), BF16),
        jax.ShapeDtypeStruct(w_out.shape, BF16),
        jax.ShapeDtypeStruct(w_gate.shape, BF16),
        jax.ShapeDtypeStruct(w_up.shape, BF16),
        jax.ShapeDtypeStruct(w_down.shape, BF16),
    )
    weight_specs = [
        pl.BlockSpec((wo_rows, D_MODEL), row),
        pl.BlockSpec((wg_rows, d_ff), row),
        pl.BlockSpec((wg_rows, d_ff), row),
        pl.BlockSpec((wd_rows, D_MODEL), lambda i: (i // 2, 0)),
    ]
    out_specs = (
        pl.BlockSpec((DA_Q_COLS, tm), lambda i: (0, i)),
        pl.BlockSpec((DA_HEADS, tm, DA_QK_PAIR), lambda i: (0, i, 0)),
        pl.BlockSpec((tm // ATT_TILE, DA_V_COLS, ATT_TILE), lambda i: (i, 0, 0)),
        pl.BlockSpec((tm, RET_Q_COLS), row),
        pl.BlockSpec((tm, RET_K_COLS), row),
        pl.BlockSpec((tm, RET_WIDTH), row),
        pl.BlockSpec((tm, RET_WIDTH), row),
        *weight_specs,
    )
    return pl.pallas_call(
        _inproj_kernel,
        grid=(steps,),
        in_specs=[
            pl.BlockSpec((tm, D_MODEL), row),
            _const_spec((1, D_MODEL)),
            _const_spec(w_in.shape),
            *weight_specs,
        ],
        out_specs=out_specs,
        out_shape=out_shapes,
        scratch_shapes=[
            pltpu.VMEM((DA_Q_COLS, D_MODEL), BF16),
            pltpu.VMEM((D_MODEL, DA_K_COLS), BF16),
            pltpu.VMEM((DA_V_COLS, D_MODEL), BF16),
            pltpu.VMEM((D_MODEL, w_in.shape[1] - DA_Q_COLS - DA_K_COLS - DA_V_COLS), BF16),
        ],
        compiler_params=pltpu.CompilerParams(
            dimension_semantics=("arbitrary",), vmem_limit_bytes=VMEM_LIMIT_BYTES),
        name="inproj",
    )(x2, g, w_in, w_out, w_gate, w_up, w_down)


def _attn_tables():
    t = ATT_TILE
    kk = np.arange(t)[:, None]
    qq = np.arange(t)[None, :]
    allowed = (kk // CHUNK) <= (qq // CHUNK)
    diag = np.empty((DA_HEADS, t, 2 * t), np.float32)
    prow = np.zeros((DA_HEADS, ATT_POS_ROWS, 2 * ATT_QTILE), np.float32)
    for h, slope in enumerate(ALIBI_SLOPES):
        c = slope * LOG2E
        d = np.where(allowed, -2.0 * c * np.maximum(kk - qq, 0), -np.inf)
        diag[h] = np.concatenate([d, d], axis=1)
        rem = c
        for j in range(POS_TERMS):
            piece = float(np.asarray(rem, np.float32).astype(jnp.bfloat16).astype(np.float32))
            prow[h, j, :] = piece * CHUNK
            prow[h, POS_TERMS + j, :] = piece
            rem = rem - piece
    return diag, prow


def _pos_features(s):
    pos = np.arange(s)
    feat = np.zeros((s, ATT_POS_ROWS), np.float32)
    for j in range(POS_TERMS):
        feat[:, j] = pos // CHUNK
        feat[:, POS_TERMS + j] = pos % CHUNK
    return feat


def _attn_kernel(lam_init, qT_ref, k_ref, pos_ref, vT_ref, diag_ref, prow_ref, lamv_ref, g_ref,
                 o_ref, acc_ref, *sp_refs):
    t = ATT_TILE
    tq = ATT_QTILE
    w = 2 * tq
    hw = w // 2
    nq = qT_ref.shape[1] // tq
    s_refs = sp_refs[:ATT_LOOKAHEAD + ATT_GROUP]
    p_refs = sp_refs[ATT_LOOKAHEAD + ATT_GROUP:]
    z = jnp.zeros((DA_QK_DIM, t), BF16)
    ones_rows = (lax.broadcasted_iota(jnp.int32, (ATT_DEN_ROWS, t), 0) == 0).astype(BF16)
    diag = diag_ref[0]
    lv = lamv_ref[...]
    lam = (jnp.exp(jnp.sum(lv[0:1] * lv[1:2], axis=-1, keepdims=True))
           - jnp.exp(jnp.sum(lv[2:3] * lv[3:4], axis=-1, keepdims=True)) + lam_init)

    def keys(k0, nk):
        rows = slice(k0 * t, (k0 + nk) * t)
        return jnp.concatenate([k_ref[rows, :], pos_ref[rows, :]], axis=1)

    def values_t(k0, nk):
        return jnp.concatenate(
            [jnp.concatenate([vT_ref[ki], ones_rows], axis=0) for ki in range(k0, k0 + nk)], axis=1)

    def colmax(a):
        return jnp.max(a, axis=0, keepdims=True)

    def prob(a, m):
        return jnp.exp2(a - m).astype(BF16)

    def combine(a):
        num = a[:DA_V_DIM] * (1.0 / a[DA_V_DIM:DA_V_DIM + 1])
        return num[:, :t] - lam * num[:, t:]

    def query_operand(qi):
        cols = []
        for half in range(2):
            qh = qT_ref[:, qi * tq + half * t:qi * tq + (half + 1) * t]
            cols.append(jnp.concatenate([qh[:DA_QK_DIM], z], axis=0))
            cols.append(jnp.concatenate([z, qh[DA_QK_DIM:]], axis=0))
        return jnp.concatenate([jnp.concatenate(cols, axis=1), prow_ref[0]], axis=0)

    tasks = []
    for qi in range(nq):
        tasks.append((qi, "first", 2 * qi))
        tasks += [(qi, "full", k0) for k0 in range(2 * qi)]
        tasks.append((qi, "last", 2 * qi + 1))

    qexts = {}

    def scores(task):
        qi, kind, k0 = task
        if qi not in qexts:
            qexts[qi] = query_operand(qi)
        qext = qexts[qi]
        if kind == "last":
            s = jnp.dot(keys(k0, 1), qext[:, hw:], preferred_element_type=F32) + diag
        else:
            s = jnp.dot(keys(k0, 1), qext, preferred_element_type=F32)
            if kind == "first":
                s = jnp.concatenate([s[:, :hw] + diag, s[:, hw:]], axis=1)
        return s

    row0 = pl.multiple_of(jnp.minimum(pl.program_id(0), 0), t)

    def issue(idx):
        s = scores(tasks[idx])
        s_refs[idx % len(s_refs)][pl.ds(row0, t), :s.shape[1]] = s
        return colmax(s)

    def probabilities(gi, group, m):
        ncol = hw if tasks[group[0]][1] == "last" else w
        p_ref = p_refs[gi % len(p_refs)]
        for j, idx in enumerate(group):
            s_ref = s_refs[idx % len(s_refs)]
            for r in range(0, t, ATT_EXP_ROWS):
                rows = pl.ds(pl.multiple_of(row0 + r, ATT_EXP_ROWS), ATT_EXP_ROWS)
                out_rows = pl.ds(pl.multiple_of(row0 + j * t + r, ATT_EXP_ROWS), ATT_EXP_ROWS)
                p_ref[out_rows, :ncol] = prob(s_ref[rows, :ncol], m)
        return p_ref[pl.ds(row0, len(group) * t), :ncol]

    groups = []
    idx = 0
    for qi in range(nq):
        groups.append([idx])
        idx += 1
        for _ in range(0, 2 * qi, ATT_GROUP):
            groups.append(list(range(idx, idx + ATT_GROUP)))
            idx += ATT_GROUP
        groups.append([idx])
        idx += 1

    smaxes = []
    m = None
    for gi, group in enumerate(groups):
        while len(smaxes) < min(group[-1] + 1 + ATT_LOOKAHEAD, len(tasks)):
            smaxes.append(issue(len(smaxes)))
        qi, kind, k0 = tasks[group[0]]
        smax = functools.reduce(jnp.maximum, [smaxes[i] for i in group])
        acc = acc_ref.at[qi % 2]
        if kind == "first":
            m = smax
            acc[...] = jnp.dot(
                values_t(k0, 1), probabilities(gi, group, m), preferred_element_type=F32)
        elif kind == "full":
            m_new = jnp.maximum(m, smax)
            pv = jnp.dot(values_t(k0, len(group)), probabilities(gi, group, m_new),
                         preferred_element_type=F32)
            acc[...] = acc[...] * jnp.exp2(m - m_new) + pv
            m = m_new
        else:
            mh = m[:, hw:]
            mh_new = jnp.maximum(mh, smax)
            pv = jnp.dot(values_t(k0, 1), probabilities(gi, group, mh_new),
                         preferred_element_type=F32)
            acc_h = acc[:, hw:] * jnp.exp2(mh - mh_new) + pv
            o = jnp.concatenate([combine(acc[:, :hw]), combine(acc_h)], axis=1)
            o = o * lax.rsqrt(jnp.mean(o * o, axis=0, keepdims=True) + NORM_EPS)
            o_ref[qi * tq:(qi + 1) * tq, :] = (
                o.T * g_ref[...] * (1.0 - lam_init)).astype(o_ref.dtype)


def _attention(qT, k, vT, lamv, subln_g, batch, seq, lam_init):
    t = ATT_TILE
    tq = ATT_QTILE
    nk = seq // t
    n = batch * seq
    diag, prow = _attn_tables()
    diag = jnp.asarray(diag)
    prow = jnp.asarray(prow, BF16)
    pos = jnp.asarray(_pos_features(seq), BF16)
    return pl.pallas_call(
        functools.partial(_attn_kernel, lam_init),
        grid=(batch, DA_HEADS),
        in_specs=[
            pl.BlockSpec((DA_QK_PAIR, seq), lambda b, h: (h, b)),
            pl.BlockSpec((None, seq, DA_QK_PAIR), lambda b, h: (h, b, 0)),
            pl.BlockSpec((seq, ATT_POS_ROWS), lambda b, h: (0, 0)),
            pl.BlockSpec((nk, DA_V_DIM, t), lambda b, h: (b, h, 0)),
            pl.BlockSpec((1, t, 2 * t), lambda b, h: (h, 0, 0)),
            pl.BlockSpec((1, ATT_POS_ROWS, 2 * tq), lambda b, h: (h, 0, 0)),
            pl.BlockSpec((4, DA_QK_DIM), lambda b, h: (0, 0)),
            pl.BlockSpec((1, DA_V_DIM), lambda b, h: (0, 0)),
        ],
        out_specs=pl.BlockSpec((seq, DA_V_DIM), lambda b, h: (b, h)),
        out_shape=jax.ShapeDtypeStruct((n, DA_WIDTH), BF16),
        scratch_shapes=[
            pltpu.VMEM((2, DA_V_DIM + ATT_DEN_ROWS, 2 * tq), F32),
            *[pltpu.VMEM((t, 2 * tq), F32) for _ in range(ATT_LOOKAHEAD + ATT_GROUP)],
            *[pltpu.VMEM((ATT_GROUP * t, 2 * tq), BF16) for _ in range(2)],
        ],
        compiler_params=pltpu.CompilerParams(
            dimension_semantics=("arbitrary", "arbitrary"),
            vmem_limit_bytes=VMEM_LIMIT_BYTES),
        name="diff_attention",
    )(qT, k, pos, vT, diag, prow, lamv, subln_g)


def _ret_tables():
    L = RET_BLOCK
    i = np.arange(L)[:, None].astype(np.float64)
    j = np.arange(L)[None, :].astype(np.float64)
    allowed = (j // CHUNK) <= (i // CHUNK)
    dmat = np.empty((RET_HEADS, L, L), np.float32)
    lanes = np.zeros((RET_HEADS, 1, RET_QK_PAIR), np.float32)
    qdec = np.empty((RET_HEADS, L, RET_V_DIM), np.float32)
    kdec = np.zeros((RET_HEADS, L, RET_QK_PAIR), np.float32)
    cdec = np.empty((RET_HEADS, 1, RET_V_DIM), np.float32)
    lane = np.arange(RET_QK_PAIR)[None, :]
    for h, gamma in enumerate(RET_GAMMA):
        mine = (lane // RET_QK_DIM) == (h % 2)
        dmat[h] = np.where(allowed, gamma ** np.abs(i - j), 0.0) * RET_QK_DIM ** -0.5
        lanes[h] = mine
        qdec[h] = gamma ** (i + 1.0)
        kdec[h] = np.where(mine, gamma ** (L - 1.0 - i), 0.0) * RET_QK_DIM ** -0.5
        cdec[h] = gamma ** L
    return dmat, lanes, qdec, kdec, cdec


def _ret_kernel(q_ref, k_ref, v_ref, gate_ref, dmat_ref, lanes_ref, qdec_ref, kdec_ref, cdec_ref,
                g_ref, o_ref):
    L = RET_BLOCK
    tasks = [(blk, h) for blk in range(q_ref.shape[0] // L) for h in range(RET_HEADS)]
    own_lanes = [jnp.broadcast_to(lanes_ref[h] > 0.0, (L, RET_QK_PAIR)) for h in range(RET_HEADS)]

    def operands(task):
        blk, h = task
        rows = slice(blk * L, (blk + 1) * L)
        pair = slice((h // 2) * RET_QK_PAIR, (h // 2 + 1) * RET_QK_PAIR)
        return rows, slice(h * RET_V_DIM, (h + 1) * RET_V_DIM), q_ref[rows, pair], k_ref[rows, pair]

    def scores(task):
        _, h = task
        _, _, q, k = operands(task)
        qm = jnp.where(own_lanes[h], q, jnp.zeros_like(q))
        s = lax.dot_general(qm, k, _NT, preferred_element_type=F32) * dmat_ref[h]
        return s.astype(BF16)

    states = [jnp.zeros((RET_QK_PAIR, RET_V_DIM), F32) for _ in range(RET_HEADS)]
    pending = [scores(task) for task in tasks[:RET_LOOKAHEAD]]
    for idx, task in enumerate(tasks):
        _, h = task
        s = pending.pop(0)
        if idx + RET_LOOKAHEAD < len(tasks):
            pending.append(scores(tasks[idx + RET_LOOKAHEAD]))
        rows, head, q, k = operands(task)
        v = v_ref[rows, head]
        inner = jnp.dot(s, v, preferred_element_type=F32)
        cross = jnp.dot(q, states[h].astype(BF16), preferred_element_type=F32) * qdec_ref[h]
        kd = (k.astype(F32) * kdec_ref[h]).astype(BF16)
        states[h] = states[h] * cdec_ref[h] + lax.dot_general(
            kd, v, _TN, preferred_element_type=F32)
        o = _rms(inner + cross) * g_ref[...]
        o_ref[rows, head] = (_silu(gate_ref[rows, head].astype(F32)) * o).astype(o_ref.dtype)


def _retention(rq, rk, rv, rg, norm_g, batch, seq):
    n = batch * seq
    tables = [jnp.asarray(a) for a in _ret_tables()]
    row = lambda b: (b, 0)
    return pl.pallas_call(
        _ret_kernel,
        grid=(batch,),
        in_specs=[
            pl.BlockSpec((seq, RET_Q_COLS), row),
            pl.BlockSpec((seq, RET_K_COLS), row),
            pl.BlockSpec((seq, RET_WIDTH), row),
            pl.BlockSpec((seq, RET_WIDTH), row),
            *[_const_spec(a.shape) for a in tables],
            _const_spec((1, RET_V_DIM)),
        ],
        out_specs=pl.BlockSpec((seq, RET_WIDTH), row),
        out_shape=jax.ShapeDtypeStruct((n, RET_WIDTH), BF16),
        compiler_params=pltpu.CompilerParams(
            dimension_semantics=("arbitrary",), vmem_limit_bytes=VMEM_LIMIT_BYTES),
        name="retention",
    )(rq, rk, rv, rg, *tables, norm_g)


def _ffn_kernel(x_ref, oa_ref, or_ref, wo_ref, g1_ref, g2_ref, wg_ref, wu_ref, wd_ref, g3_ref,
                out_ref):
    subs = [slice(j * FFN_SUB, (j + 1) * FFN_SUB) for j in range(x_ref.shape[0] // FFN_SUB)]
    mix = [jnp.dot(oa_ref[r, :], wo_ref[:DA_WIDTH, :], preferred_element_type=F32)
           + jnp.dot(or_ref[r, :], wo_ref[DA_WIDTH:, :], preferred_element_type=F32) for r in subs]
    x1 = [x_ref[r, :] + _rms(mx) * g1_ref[...] for r, mx in zip(subs, mix)]
    h = [(_rms(a) * g2_ref[...]).astype(BF16) for a in x1]
    gate = [jnp.dot(a, wg_ref[...], preferred_element_type=F32) for a in h]
    up = [jnp.dot(a, wu_ref[...], preferred_element_type=F32) for a in h]
    f = [(_silu(gt) * u).astype(BF16) for gt, u in zip(gate, up)]
    y = [jnp.dot(a, wd_ref[...], preferred_element_type=F32) for a in f]
    for r, a, b in zip(subs, x1, y):
        out_ref[r, :] = a + _rms(b) * g3_ref[...]


def _ffn(x2, oa, orr, wo, g1, g2, wg, wu, wd, g3):
    n = x2.shape[0]
    tm = ROW_TILE
    row = lambda i: (i, 0)
    return pl.pallas_call(
        _ffn_kernel,
        grid=(n // tm,),
        in_specs=[
            pl.BlockSpec((tm, D_MODEL), row),
            pl.BlockSpec((tm, DA_WIDTH), row),
            pl.BlockSpec((tm, RET_WIDTH), row),
            _const_spec(wo.shape),
            _const_spec((1, D_MODEL)),
            _const_spec((1, D_MODEL)),
            _const_spec(wg.shape),
            _const_spec(wu.shape),
            _const_spec(wd.shape),
            _const_spec((1, D_MODEL)),
        ],
        out_specs=pl.BlockSpec((tm, D_MODEL), row),
        out_shape=jax.ShapeDtypeStruct((n, D_MODEL), F32),
        compiler_params=pltpu.CompilerParams(
            dimension_semantics=("arbitrary",), vmem_limit_bytes=VMEM_LIMIT_BYTES),
        name="outproj_ffn",
    )(x2, oa, orr, wo, g1, g2, wg, wu, wd, g3)


def kernel(x, pre_mix_g, w_in, lambda_q1, lambda_k1, lambda_q2, lambda_k2, da_subln_g, ret_norm_g,
           w_out, post_mix_g, pre_ffn_g, w_gate, w_up, w_down, post_ffn_g):
    batch, seq, d = x.shape
    depth = w_in.shape[0]
    assert d == D_MODEL and seq % ATT_QTILE == 0 and seq % RET_BLOCK == 0
    assert ATT_QTILE == 2 * ATT_TILE and ATT_TILE % CHUNK == 0 and RET_BLOCK % CHUNK == 0
    assert DA_QK_PAIR == V7X_LANES and RET_QK_PAIR == V7X_LANES
    assert DA_V_DIM == V7X_LANES and RET_V_DIM == V7X_LANES and ATT_TILE == V7X_MXU_DIM
    assert (batch * seq) % ROW_TILE == 0 and ROW_TILE % FFN_SUB == 0
    assert (batch * seq) % INPROJ_TILE == 0 and INPROJ_TILE % ATT_TILE == 0
    x2 = x.reshape(batch * seq, d)
    for l in range(depth):
        qT, k, vT, rq, rk, rv, rg, wo, wg, wu, wd = _inproj(
            x2, pre_mix_g[l][None, :], w_in[l], w_out[l], w_gate[l], w_up[l], w_down[l])
        lam_init = 0.8 - 0.6 * math.exp(-0.3 * l)
        lamv = jnp.stack([lambda_q1[l], lambda_k1[l], lambda_q2[l], lambda_k2[l]]).astype(F32)
        o_da = _attention(qT, k, vT, lamv, da_subln_g[l][None, :].astype(F32), batch, seq, lam_init)
        o_ret = _retention(rq, rk, rv, rg, ret_norm_g[l][None, :].astype(F32), batch, seq)
        x2 = _ffn(x2, o_da, o_ret, wo, post_mix_g[l][None, :], pre_ffn_g[l][None, :],
                  wg, wu, wd, post_ffn_g[l][None, :])
    return x2.reshape(batch, seq, d)
```

```python
import functools
import math

import numpy as np
import jax
import jax.numpy as jnp
from jax import lax
from jax.experimental import pallas as pl
from jax.experimental.pallas import tpu as pltpu

F32 = jnp.float32
BF16 = jnp.bfloat16

D_MODEL = 1024
CHUNK = 64
NORM_EPS = 1e-6
DA_HEADS = 4
DA_QK_DIM = 64
DA_V_DIM = 128
DA_WIDTH = DA_HEADS * DA_V_DIM
ALIBI_SLOPES = tuple(2.0 ** (-8.0 * (h + 1) / DA_HEADS) for h in range(DA_HEADS))
RET_HEADS = 4
RET_QK_DIM = 64
RET_V_DIM = 128
RET_WIDTH = RET_HEADS * RET_V_DIM
RET_GAMMA = tuple(1.0 - 2.0 ** (-5.0 - h) for h in range(RET_HEADS))
DA_Q_COLS = DA_HEADS * 2 * DA_QK_DIM
DA_K_COLS = DA_HEADS * 2 * DA_QK_DIM
DA_V_COLS = DA_WIDTH
RET_Q_COLS = RET_HEADS * RET_QK_DIM
RET_K_COLS = RET_HEADS * RET_QK_DIM
DA_QK_PAIR = 2 * DA_QK_DIM
RET_QK_PAIR = 2 * RET_QK_DIM

V7X_LANES = 128
V7X_BF16_SUBLANES = 16
V7X_MXU_DIM = 256
V7X_VMEM_BYTES = 64 * 1024 * 1024
VMEM_LIMIT_BYTES = V7X_VMEM_BYTES - 8 * 1024 * 1024

INPROJ_TILE = 1024
ROW_TILE = 1024
FFN_SUB = 256
ATT_TILE = 256
ATT_QTILE = 256
ATT_STEP_HEADS = 2
ATT_DEN_ROWS = 16
ATT_GROUP = 1
ATT_LOOKAHEAD = 5
ATT_EXP_ROWS = 32
ATT_POS_ROWS = V7X_MXU_DIM - DA_QK_PAIR
RET_BLOCK = 256
RET_LOOKAHEAD = 2
LOG2E = math.log2(math.e)
POS_TERMS = 3

_NT = (((1,), (1,)), ((), ()))
_TN = (((0,), (0,)), ((), ()))


def _rms(x):
    return x * lax.rsqrt(jnp.mean(x * x, axis=-1, keepdims=True) + NORM_EPS)


def _silu(x):
    hx = 0.5 * x
    return hx * (jnp.tanh(hx) + 1.0)


def _const_spec(shape):
    nd = len(shape)
    return pl.BlockSpec(shape, lambda *_: (0,) * nd, pipeline_mode=pl.Buffered(1))


def _inproj_kernel(x_ref, g_ref, win_ref, wo32_ref, wg32_ref, wu32_ref, wd32_ref,
                   qT_ref, k_ref, vT_ref, rq_ref, rk_ref, rv_ref, rg_ref,
                   wo_ref, wg_ref, wu_ref, wd_ref,
                   wqT_ref, wk_ref, wvT_ref, wr_ref):
    c1 = DA_Q_COLS
    c2 = c1 + DA_K_COLS
    c3 = c2 + DA_V_COLS

    @pl.when(pl.program_id(0) == 0)
    def _():
        for c in range(0, DA_Q_COLS, V7X_LANES):
            cols = slice(c, c + V7X_LANES)
            wqT_ref[cols, :] = win_ref[:, cols].T.astype(BF16)
            wvT_ref[cols, :] = win_ref[:, c2 + c:c2 + c + V7X_LANES].T.astype(BF16)
        wk_ref[...] = win_ref[:, c1:c2].astype(BF16)
        wr_ref[...] = win_ref[:, c3:].astype(BF16)

    wo_ref[...] = wo32_ref[...].astype(BF16)
    wg_ref[...] = wg32_ref[...].astype(BF16)
    wu_ref[...] = wu32_ref[...].astype(BF16)
    wd_ref[...] = wd32_ref[...].astype(BF16)

    for j in range(INPROJ_TILE // ATT_TILE):
        rows = slice(j * ATT_TILE, (j + 1) * ATT_TILE)
        h = (_rms(x_ref[rows, :]) * g_ref[...]).astype(BF16)
        qT = lax.dot_general(wqT_ref[...], h, _NT, preferred_element_type=F32)
        qT_ref[:, rows] = (qT * (DA_QK_DIM ** -0.5 * LOG2E)).astype(BF16)
        k = jnp.dot(h, wk_ref[...], preferred_element_type=F32)
        for hh in range(DA_HEADS):
            k_ref[hh, rows, :] = k[:, hh * DA_QK_PAIR:(hh + 1) * DA_QK_PAIR].astype(BF16)
        vT = lax.dot_general(wvT_ref[...], h, _NT, preferred_element_type=F32)
        vT_ref[j] = vT.astype(BF16)
        r = jnp.dot(h, wr_ref[...], preferred_element_type=F32)
        r1 = RET_Q_COLS
        r2 = r1 + RET_K_COLS
        r3 = r2 + RET_WIDTH
        rq_ref[rows, :] = r[:, :r1].astype(BF16)
        rk_ref[rows, :] = r[:, r1:r2].astype(BF16)
        rv_ref[rows, :] = r[:, r2:r3].astype(BF16)
        rg_ref[rows, :] = r[:, r3:].astype(BF16)


def _inproj(x2, g, w_in, w_out, w_gate, w_up, w_down):
    n = x2.shape[0]
    tm = INPROJ_TILE
    steps = n // tm
    d_ff = w_gate.shape[1]
    wo_rows, wg_rows, wd_rows = D_MODEL // steps, D_MODEL // steps, d_ff // (steps // 2)
    assert wo_rows * steps == D_MODEL and wd_rows * (steps // 2) == d_ff
    assert wo_rows % V7X_BF16_SUBLANES == 0 and wd_rows % V7X_BF16_SUBLANES == 0
    row = lambda i: (i, 0)
    out_shapes = (
        jax.ShapeDtypeStruct((DA_Q_COLS, n), BF16),
        jax.ShapeDtypeStruct((DA_HEADS, n, DA_QK_PAIR), BF16),
        jax.ShapeDtypeStruct((n // ATT_TILE, DA_V_COLS, ATT_TILE), BF16),
        jax.ShapeDtypeStruct((n, RET_Q_COLS), BF16),
        jax.ShapeDtypeStruct((n, RET_K_COLS), BF16),
        jax.ShapeDtypeStruct((n, RET_WIDTH), BF16),
        jax.ShapeDtypeStruct((n, RET_WIDTH), BF16),
        jax.ShapeDtypeStruct(w_out.shape, BF16),
        jax.ShapeDtypeStruct(w_gate.shape, BF16),
        jax.ShapeDtypeStruct(w_up.shape, BF16),
        jax.ShapeDtypeStruct(w_down.shape, BF16),
    )
    weight_specs = [
        pl.BlockSpec((wo_rows, D_MODEL), row),
        pl.BlockSpec((wg_rows, d_ff), row),
        pl.BlockSpec((wg_rows, d_ff), row),
        pl.BlockSpec((wd_rows, D_MODEL), lambda i: (i // 2, 0)),
    ]
    out_specs = (
        pl.BlockSpec((DA_Q_COLS, tm), lambda i: (0, i)),
        pl.BlockSpec((DA_HEADS, tm, DA_QK_PAIR), lambda i: (0, i, 0)),
        pl.BlockSpec((tm // ATT_TILE, DA_V_COLS, ATT_TILE), lambda i: (i, 0, 0)),
        pl.BlockSpec((tm, RET_Q_COLS), row),
        pl.BlockSpec((tm, RET_K_COLS), row),
        pl.BlockSpec((tm, RET_WIDTH), row),
        pl.BlockSpec((tm, RET_WIDTH), row),
        *weight_specs,
    )
    return pl.pallas_call(
        _inproj_kernel,
        grid=(steps,),
        in_specs=[
            pl.BlockSpec((tm, D_MODEL), row),
            _const_spec((1, D_MODEL)),
            _const_spec(w_in.shape),
            *weight_specs,
        ],
        out_specs=out_specs,
        out_shape=out_shapes,
        scratch_shapes=[
            pltpu.VMEM((DA_Q_COLS, D_MODEL), BF16),
            pltpu.VMEM((D_MODEL, DA_K_COLS), BF16),
            pltpu.VMEM((DA_V_COLS, D_MODEL), BF16),
            pltpu.VMEM((D_MODEL, w_in.shape[1] - DA_Q_COLS - DA_K_COLS - DA_V_COLS), BF16),
        ],
        compiler_params=pltpu.CompilerParams(
            dimension_semantics=("arbitrary",), vmem_limit_bytes=VMEM_LIMIT_BYTES),
        name="inproj",
    )(x2, g, w_in, w_out, w_gate, w_up, w_down)


def _attn_tables():
    t = ATT_TILE
    kk = np.arange(t)[:, None]
    qq = np.arange(t)[None, :]
    allowed = (kk // CHUNK) <= (qq // CHUNK)
    diag = np.empty((DA_HEADS, t, 2 * t), np.float32)
    prow = np.zeros((DA_HEADS, ATT_POS_ROWS, 2 * ATT_QTILE), np.float32)
    for h, slope in enumerate(ALIBI_SLOPES):
        c = slope * LOG2E
        d = np.where(allowed, -2.0 * c * np.maximum(kk - qq, 0), -np.inf)
        diag[h] = np.concatenate([d, d], axis=1)
        rem = c
        for j in range(POS_TERMS):
            piece = float(np.asarray(rem, np.float32).astype(jnp.bfloat16).astype(np.float32))
            prow[h, j, :] = piece * CHUNK
            prow[h, POS_TERMS + j, :] = piece
            rem = rem - piece
    return diag, prow


def _pos_features(s):
    pos = np.arange(s)
    feat = np.zeros((s, ATT_POS_ROWS), np.float32)
    for j in range(POS_TERMS):
        feat[:, j] = pos // CHUNK
        feat[:, POS_TERMS + j] = pos % CHUNK
    return feat


def _attn_kernel(lam_init, qT_ref, k_ref, pos_ref, vT_ref, diag_ref, prow_ref, lamv_ref, g_ref,
                 o_ref, *scratch):
    per_head = len(scratch) // ATT_STEP_HEADS
    for hh in range(ATT_STEP_HEADS):
        lanes = pl.ds(hh * DA_V_DIM, DA_V_DIM)
        _attn_head(lam_init, qT_ref.at[lanes, :], k_ref.at[hh], pos_ref, vT_ref.at[:, lanes, :],
                   diag_ref.at[hh], prow_ref.at[hh], lamv_ref, g_ref, o_ref.at[:, lanes],
                   *scratch[hh * per_head:(hh + 1) * per_head])


def _attn_head(lam_init, qT_ref, k_ref, pos_ref, vT_ref, diag_ref, prow_ref, lamv_ref, g_ref,
               o_ref, acc_ref, *sp_refs):
    t = ATT_TILE
    tq = ATT_QTILE
    w = 2 * tq
    nq = qT_ref.shape[1] // tq
    s_refs = sp_refs[:ATT_LOOKAHEAD + ATT_GROUP]
    p_refs = sp_refs[ATT_LOOKAHEAD + ATT_GROUP:]
    z = jnp.zeros((DA_QK_DIM, t), BF16)
    ones_rows = (lax.broadcasted_iota(jnp.int32, (ATT_DEN_ROWS, t), 0) == 0).astype(BF16)
    diag = diag_ref[...]
    lv = lamv_ref[...]
    lam = (jnp.exp(jnp.sum(lv[0:1] * lv[1:2], axis=-1, keepdims=True))
           - jnp.exp(jnp.sum(lv[2:3] * lv[3:4], axis=-1, keepdims=True)) + lam_init)

    def keys(k0, nk):
        rows = slice(k0 * t, (k0 + nk) * t)
        return jnp.concatenate([k_ref[rows, :], pos_ref[rows, :]], axis=1)

    def values_t(k0, nk):
        return jnp.concatenate(
            [jnp.concatenate([vT_ref[ki], ones_rows], axis=0) for ki in range(k0, k0 + nk)], axis=1)

    def colmax(a):
        return jnp.max(a, axis=0, keepdims=True)

    def prob(a, m):
        return jnp.exp2(a - m).astype(BF16)

    def combine(a):
        num = a[:DA_V_DIM] * (1.0 / a[DA_V_DIM:DA_V_DIM + 1])
        return num[:, :t] - lam * num[:, t:]

    def query_operand(qi):
        qh = qT_ref[:, qi * tq:(qi + 1) * tq]
        cols = [jnp.concatenate([qh[:DA_QK_DIM], z], axis=0),
                jnp.concatenate([z, qh[DA_QK_DIM:]], axis=0)]
        return jnp.concatenate([jnp.concatenate(cols, axis=1), prow_ref[...]], axis=0)

    tasks = []
    for qi in range(nq):
        tasks.append((qi, "diag", qi))
        tasks += [(qi, "full", k0) for k0 in range(qi)]

    qexts = {}

    def scores(task):
        qi, kind, k0 = task
        if qi not in qexts:
            qexts[qi] = query_operand(qi)
        s = jnp.dot(keys(k0, 1), qexts[qi], preferred_element_type=F32)
        return s + diag if kind == "diag" else s

    row0 = pl.multiple_of(jnp.minimum(pl.program_id(0), 0), t)

    def issue(idx):
        s = scores(tasks[idx])
        s_refs[idx % len(s_refs)][pl.ds(row0, t), :s.shape[1]] = s
        return colmax(s)

    def probabilities(gi, group, m):
        ncol = w
        p_ref = p_refs[gi % len(p_refs)]
        for j, idx in enumerate(group):
            s_ref = s_refs[idx % len(s_refs)]
            for r in range(0, t, ATT_EXP_ROWS):
                rows = pl.ds(pl.multiple_of(row0 + r, ATT_EXP_ROWS), ATT_EXP_ROWS)
                out_rows = pl.ds(pl.multiple_of(row0 + j * t + r, ATT_EXP_ROWS), ATT_EXP_ROWS)
                p_ref[out_rows, :ncol] = prob(s_ref[rows, :ncol], m)
        return p_ref[pl.ds(row0, len(group) * t), :ncol]

    groups = []
    idx = 0
    for qi in range(nq):
        groups.append([idx])
        idx += 1
        for g0 in range(0, qi, ATT_GROUP):
            size = min(ATT_GROUP, qi - g0)
            groups.append(list(range(idx, idx + size)))
            idx += size

    smaxes = []
    m = None
    for gi, group in enumerate(groups):
        while len(smaxes) < min(group[-1] + 1 + ATT_LOOKAHEAD, len(tasks)):
            smaxes.append(issue(len(smaxes)))
        qi, kind, k0 = tasks[group[0]]
        smax = functools.reduce(jnp.maximum, [smaxes[i] for i in group])
        acc = acc_ref.at[qi % 2]
        if kind == "diag":
            m = smax
            acc[...] = jnp.dot(
                values_t(k0, 1), probabilities(gi, group, m), preferred_element_type=F32)
        else:
            m_new = jnp.maximum(m, smax)
            pv = jnp.dot(values_t(k0, len(group)), probabilities(gi, group, m_new),
                         preferred_element_type=F32)
            acc[...] = acc[...] * jnp.exp2(m - m_new) + pv
            m = m_new
        if gi + 1 == len(groups) or tasks[groups[gi + 1][0]][0] != qi:
            o = combine(acc[...])
            o = o * lax.rsqrt(jnp.mean(o * o, axis=0, keepdims=True) + NORM_EPS)
            o_ref[qi * tq:(qi + 1) * tq, :] = (
                o.T * g_ref[...] * (1.0 - lam_init)).astype(o_ref.dtype)


def _attention(qT, k, vT, lamv, subln_g, batch, seq, lam_init):
    t = ATT_TILE
    tq = ATT_QTILE
    nk = seq // t
    n = batch * seq
    diag, prow = _attn_tables()
    diag = jnp.asarray(diag)
    prow = jnp.asarray(prow, BF16)
    pos = jnp.asarray(_pos_features(seq), BF16)
    sh = ATT_STEP_HEADS
    head_scratch = [
        pltpu.VMEM((2, DA_V_DIM + ATT_DEN_ROWS, 2 * tq), F32),
        *[pltpu.VMEM((t, 2 * tq), F32) for _ in range(ATT_LOOKAHEAD + ATT_GROUP)],
        *[pltpu.VMEM((ATT_GROUP * t, 2 * tq), BF16) for _ in range(ATT_LOOKAHEAD + 1)],
    ]
    return pl.pallas_call(
        functools.partial(_attn_kernel, lam_init),
        grid=(batch, DA_HEADS // sh),
        in_specs=[
            pl.BlockSpec((sh * DA_QK_PAIR, seq), lambda b, h: (h, b)),
            pl.BlockSpec((sh, seq, DA_QK_PAIR), lambda b, h: (h, b, 0)),
            pl.BlockSpec((seq, ATT_POS_ROWS), lambda b, h: (0, 0)),
            pl.BlockSpec((nk, sh * DA_V_DIM, t), lambda b, h: (b, h, 0)),
            pl.BlockSpec((sh, t, 2 * t), lambda b, h: (h, 0, 0)),
            pl.BlockSpec((sh, ATT_POS_ROWS, 2 * tq), lambda b, h: (h, 0, 0)),
            pl.BlockSpec((4, DA_QK_DIM), lambda b, h: (0, 0)),
            pl.BlockSpec((1, DA_V_DIM), lambda b, h: (0, 0)),
        ],
        out_specs=pl.BlockSpec((seq, sh * DA_V_DIM), lambda b, h: (b, h)),
        out_shape=jax.ShapeDtypeStruct((n, DA_WIDTH), BF16),
        scratch_shapes=head_scratch * sh,
        compiler_params=pltpu.CompilerParams(
            dimension_semantics=("arbitrary", "arbitrary"),
            vmem_limit_bytes=VMEM_LIMIT_BYTES),
        name="diff_attention",
    )(qT, k, pos, vT, diag, prow, lamv, subln_g)


def _ret_tables():
    L = RET_BLOCK
    i = np.arange(L)[:, None].astype(np.float64)
    j = np.arange(L)[None, :].astype(np.float64)
    allowed = (j // CHUNK) <= (i // CHUNK)
    dmat = np.empty((RET_HEADS, L, L), np.float32)
    lanes = np.zeros((RET_HEADS, 1, RET_QK_PAIR), np.float32)
    qdec = np.empty((RET_HEADS, L, RET_V_DIM), np.float32)
    kdec = np.zeros((RET_HEADS, L, RET_QK_PAIR), np.float32)
    cdec = np.empty((RET_HEADS, 1, RET_V_DIM), np.float32)
    lane = np.arange(RET_QK_PAIR)[None, :]
    for h, gamma in enumerate(RET_GAMMA):
        mine = (lane // RET_QK_DIM) == (h % 2)
        dmat[h] = np.where(allowed, gamma ** np.abs(i - j), 0.0) * RET_QK_DIM ** -0.5
        lanes[h] = mine
        qdec[h] = gamma ** (i + 1.0)
        kdec[h] = np.where(mine, gamma ** (L - 1.0 - i), 0.0) * RET_QK_DIM ** -0.5
        cdec[h] = gamma ** L
    return dmat, lanes, qdec, kdec, cdec


def _ret_kernel(q_ref, k_ref, v_ref, gate_ref, dmat_ref, lanes_ref, qdec_ref, kdec_ref, cdec_ref,
                g_ref, o_ref):
    L = RET_BLOCK
    tasks = [(blk, h) for blk in range(q_ref.shape[0] // L) for h in range(RET_HEADS)]
    own_lanes = [jnp.broadcast_to(lanes_ref[h] > 0.0, (L, RET_QK_PAIR)) for h in range(RET_HEADS)]

    def operands(task):
        blk, h = task
        rows = slice(blk * L, (blk + 1) * L)
        pair = slice((h // 2) * RET_QK_PAIR, (h // 2 + 1) * RET_QK_PAIR)
        return rows, slice(h * RET_V_DIM, (h + 1) * RET_V_DIM), q_ref[rows, pair], k_ref[rows, pair]

    def scores(task):
        _, h = task
        _, _, q, k = operands(task)
        qm = jnp.where(own_lanes[h], q, jnp.zeros_like(q))
        s = lax.dot_general(qm, k, _NT, preferred_element_type=F32) * dmat_ref[h]
        return s.astype(BF16)

    states = [jnp.zeros((RET_QK_PAIR, RET_V_DIM), F32) for _ in range(RET_HEADS)]
    pending = [scores(task) for task in tasks[:RET_LOOKAHEAD]]
    for idx, task in enumerate(tasks):
        _, h = task
        s = pending.pop(0)
        if idx + RET_LOOKAHEAD < len(tasks):
            pending.append(scores(tasks[idx + RET_LOOKAHEAD]))
        rows, head, q, k = operands(task)
        v = v_ref[rows, head]
        inner = jnp.dot(s, v, preferred_element_type=F32)
        cross = jnp.dot(q, states[h].astype(BF16), preferred_element_type=F32) * qdec_ref[h]
        kd = (k.astype(F32) * kdec_ref[h]).astype(BF16)
        states[h] = states[h] * cdec_ref[h] + lax.dot_general(
            kd, v, _TN, preferred_element_type=F32)
        o = _rms(inner + cross) * g_ref[...]
        o_ref[rows, head] = (_silu(gate_ref[rows, head].astype(F32)) * o).astype(o_ref.dtype)


def _retention(rq, rk, rv, rg, norm_g, batch, seq):
    n = batch * seq
    tables = [jnp.asarray(a) for a in _ret_tables()]
    row = lambda b: (b, 0)
    return pl.pallas_call(
        _ret_kernel,
        grid=(batch,),
        in_specs=[
            pl.BlockSpec((seq, RET_Q_COLS), row),
            pl.BlockSpec((seq, RET_K_COLS), row),
            pl.BlockSpec((seq, RET_WIDTH), row),
            pl.BlockSpec((seq, RET_WIDTH), row),
            *[_const_spec(a.shape) for a in tables],
            _const_spec((1, RET_V_DIM)),
        ],
        out_specs=pl.BlockSpec((seq, RET_WIDTH), row),
        out_shape=jax.ShapeDtypeStruct((n, RET_WIDTH), BF16),
        compiler_params=pltpu.CompilerParams(
            dimension_semantics=("arbitrary",), vmem_limit_bytes=VMEM_LIMIT_BYTES),
        name="retention",
    )(rq, rk, rv, rg, *tables, norm_g)


def _ffn_kernel(x_ref, oa_ref, or_ref, wo_ref, g1_ref, g2_ref, wg_ref, wu_ref, wd_ref, g3_ref,
                out_ref):
    subs = [slice(j * FFN_SUB, (j + 1) * FFN_SUB) for j in range(x_ref.shape[0] // FFN_SUB)]
    mix = [jnp.dot(oa_ref[r, :], wo_ref[:DA_WIDTH, :], preferred_element_type=F32)
           + jnp.dot(or_ref[r, :], wo_ref[DA_WIDTH:, :], preferred_element_type=F32) for r in subs]
    x1 = [x_ref[r, :] + _rms(mx) * g1_ref[...] for r, mx in zip(subs, mix)]
    h = [(_rms(a) * g2_ref[...]).astype(BF16) for a in x1]
    gate = [jnp.dot(a, wg_ref[...], preferred_element_type=F32) for a in h]
    up = [jnp.dot(a, wu_ref[...], preferred_element_type=F32) for a in h]
    f = [(_silu(gt) * u).astype(BF16) for gt, u in zip(gate, up)]
    y = [jnp.dot(a, wd_ref[...], preferred_element_type=F32) for a in f]
    for r, a, b in zip(subs, x1, y):
        out_ref[r, :] = a + _rms(b) * g3_ref[...]


def _ffn(x2, oa, orr, wo, g1, g2, wg, wu, wd, g3):
    n = x2.shape[0]
    tm = ROW_TILE
    row = lambda i: (i, 0)
    return pl.pallas_call(
        _ffn_kernel,
        grid=(n // tm,),
        in_specs=[
            pl.BlockSpec((tm, D_MODEL), row),
            pl.BlockSpec((tm, DA_WIDTH), row),
            pl.BlockSpec((tm, RET_WIDTH), row),
            _const_spec(wo.shape),
            _const_spec((1, D_MODEL)),
            _const_spec((1, D_MODEL)),
            _const_spec(wg.shape),
            _const_spec(wu.shape),
            _const_spec(wd.shape),
            _const_spec((1, D_MODEL)),
        ],
        out_specs=pl.BlockSpec((tm, D_MODEL), row),
        out_shape=jax.ShapeDtypeStruct((n, D_MODEL), F32),
        compiler_params=pltpu.CompilerParams(
            dimension_semantics=("arbitrary",), vmem_limit_bytes=VMEM_LIMIT_BYTES),
        name="outproj_ffn",
    )(x2, oa, orr, wo, g1, g2, wg, wu, wd, g3)


def kernel(x, pre_mix_g, w_in, lambda_q1, lambda_k1, lambda_q2, lambda_k2, da_subln_g, ret_norm_g,
           w_out, post_mix_g, pre_ffn_g, w_gate, w_up, w_down, post_ffn_g):
    batch, seq, d = x.shape
    depth = w_in.shape[0]
    assert d == D_MODEL and seq % ATT_QTILE == 0 and seq % RET_BLOCK == 0
    assert ATT_QTILE == ATT_TILE and ATT_TILE % CHUNK == 0 and RET_BLOCK % CHUNK == 0
    assert DA_QK_PAIR == V7X_LANES and RET_QK_PAIR == V7X_LANES
    assert DA_V_DIM == V7X_LANES and RET_V_DIM == V7X_LANES and ATT_TILE == V7X_MXU_DIM
    assert (batch * seq) % ROW_TILE == 0 and ROW_TILE % FFN_SUB == 0
    assert (batch * seq) % INPROJ_TILE == 0 and INPROJ_TILE % ATT_TILE == 0
    x2 = x.reshape(batch * seq, d)
    for l in range(depth):
        qT, k, vT, rq, rk, rv, rg, wo, wg, wu, wd = _inproj(
            x2, pre_mix_g[l][None, :], w_in[l], w_out[l], w_gate[l], w_up[l], w_down[l])
        lam_init = 0.8 - 0.6 * math.exp(-0.3 * l)
        lamv = jnp.stack([lambda_q1[l], lambda_k1[l], lambda_q2[l], lambda_k2[l]]).astype(F32)
        o_da = _attention(qT, k, vT, lamv, da_subln_g[l][None, :].astype(F32), batch, seq, lam_init)
        o_ret = _retention(rq, rk, rv, rg, ret_norm_g[l][None, :].astype(F32), batch, seq)
        x2 = _ffn(x2, o_da, o_ret, wo, post_mix_g[l][None, :], pre_ffn_g[l][None, :],
                  wg, wu, wd, post_ffn_g[l][None, :])
    return x2.reshape(batch, seq, d)
```

```python
import functools
import math

import numpy as np
import jax
import jax.numpy as jnp
from jax import lax
from jax.experimental import pallas as pl
from jax.experimental.pallas import tpu as pltpu

F32 = jnp.float32
BF16 = jnp.bfloat16

D_MODEL = 1024
CHUNK = 64
NORM_EPS = 1e-6
DA_HEADS = 4
DA_QK_DIM = 64
DA_V_DIM = 128
DA_WIDTH = DA_HEADS * DA_V_DIM
ALIBI_SLOPES = tuple(2.0 ** (-8.0 * (h + 1) / DA_HEADS) for h in range(DA_HEADS))
RET_HEADS = 4
RET_QK_DIM = 64
RET_V_DIM = 128
RET_WIDTH = RET_HEADS * RET_V_DIM
RET_GAMMA = tuple(1.0 - 2.0 ** (-5.0 - h) for h in range(RET_HEADS))
DA_Q_COLS = DA_HEADS * 2 * DA_QK_DIM
DA_K_COLS = DA_HEADS * 2 * DA_QK_DIM
DA_V_COLS = DA_WIDTH
RET_Q_COLS = RET_HEADS * RET_QK_DIM
RET_K_COLS = RET_HEADS * RET_QK_DIM
DA_QK_PAIR = 2 * DA_QK_DIM
RET_QK_PAIR = 2 * RET_QK_DIM

V7X_LANES = 128
V7X_BF16_SUBLANES = 16
V7X_MXU_DIM = 256
V7X_VMEM_BYTES = 64 * 1024 * 1024
VMEM_LIMIT_BYTES = V7X_VMEM_BYTES - 8 * 1024 * 1024

INPROJ_TILE = 1024
ROW_TILE = 1024
FFN_SUB = 256
ATT_TILE = 256
ATT_QTILE = 256
ATT_DEN_ROWS = 16
ATT_GROUP = 1
ATT_LOOKAHEAD = 5
ATT_EXP_ROWS = 32
ATT_POS_ROWS = V7X_MXU_DIM - DA_QK_PAIR
RET_BLOCK = 256
RET_LOOKAHEAD = 2
LOG2E = math.log2(math.e)
POS_TERMS = 3

_NT = (((1,), (1,)), ((), ()))
_TN = (((0,), (0,)), ((), ()))


def _rms(x):
    return x * lax.rsqrt(jnp.mean(x * x, axis=-1, keepdims=True) + NORM_EPS)


def _silu(x):
    hx = 0.5 * x
    return hx * (jnp.tanh(hx) + 1.0)


def _const_spec(shape):
    nd = len(shape)
    return pl.BlockSpec(shape, lambda *_: (0,) * nd, pipeline_mode=pl.Buffered(1))


def _inproj_kernel(x_ref, g_ref, win_ref, wo32_ref, wg32_ref, wu32_ref, wd32_ref,
                   qT_ref, k_ref, vT_ref, rq_ref, rk_ref, rv_ref, rg_ref,
                   wo_ref, wg_ref, wu_ref, wd_ref,
                   wqT_ref, wk_ref, wvT_ref, wr_ref):
    c1 = DA_Q_COLS
    c2 = c1 + DA_K_COLS
    c3 = c2 + DA_V_COLS

    @pl.when(pl.program_id(0) == 0)
    def _():
        for c in range(0, DA_Q_COLS, V7X_LANES):
            cols = slice(c, c + V7X_LANES)
            wqT_ref[cols, :] = win_ref[:, cols].T.astype(BF16)
            wvT_ref[cols, :] = win_ref[:, c2 + c:c2 + c + V7X_LANES].T.astype(BF16)
        wk_ref[...] = win_ref[:, c1:c2].astype(BF16)
        wr_ref[...] = win_ref[:, c3:].astype(BF16)

    wo_ref[...] = wo32_ref[...].astype(BF16)
    wg_ref[...] = wg32_ref[...].astype(BF16)
    wu_ref[...] = wu32_ref[...].astype(BF16)
    wd_ref[...] = wd32_ref[...].astype(BF16)

    for j in range(INPROJ_TILE // ATT_TILE):
        rows = slice(j * ATT_TILE, (j + 1) * ATT_TILE)
        h = (_rms(x_ref[rows, :]) * g_ref[...]).astype(BF16)
        qT = lax.dot_general(wqT_ref[...], h, _NT, preferred_element_type=F32)
        qT_ref[:, rows] = (qT * (DA_QK_DIM ** -0.5 * LOG2E)).astype(BF16)
        k = jnp.dot(h, wk_ref[...], preferred_element_type=F32)
        for hh in range(DA_HEADS):
            k_ref[hh, rows, :] = k[:, hh * DA_QK_PAIR:(hh + 1) * DA_QK_PAIR].astype(BF16)
        vT = lax.dot_general(wvT_ref[...], h, _NT, preferred_element_type=F32)
        vT_ref[j] = vT.astype(BF16)
        r = jnp.dot(h, wr_ref[...], preferred_element_type=F32)
        r1 = RET_Q_COLS
        r2 = r1 + RET_K_COLS
        r3 = r2 + RET_WIDTH
        rq_ref[rows, :] = r[:, :r1].astype(BF16)
        rk_ref[rows, :] = r[:, r1:r2].astype(BF16)
        rv_ref[rows, :] = r[:, r2:r3].astype(BF16)
        rg_ref[rows, :] = r[:, r3:].astype(BF16)


def _inproj(x2, g, w_in, w_out, w_gate, w_up, w_down):
    n = x2.shape[0]
    tm = INPROJ_TILE
    steps = n // tm
    d_ff = w_gate.shape[1]
    wo_rows, wg_rows, wd_rows = D_MODEL // steps, D_MODEL // steps, d_ff // (steps // 2)
    assert wo_rows * steps == D_MODEL and wd_rows * (steps // 2) == d_ff
    assert wo_rows % V7X_BF16_SUBLANES == 0 and wd_rows % V7X_BF16_SUBLANES == 0
    row = lambda i: (i, 0)
    out_shapes = (
        jax.ShapeDtypeStruct((DA_Q_COLS, n), BF16),
        jax.ShapeDtypeStruct((DA_HEADS, n, DA_QK_PAIR), BF16),
        jax.ShapeDtypeStruct((n // ATT_TILE, DA_V_COLS, ATT_TILE), BF16),
        jax.ShapeDtypeStruct((n, RET_Q_COLS), BF16),
        jax.ShapeDtypeStruct((n, RET_K_COLS), BF16),
        jax.ShapeDtypeStruct((n, RET_WIDTH), BF16),
        jax.ShapeDtypeStruct((n, RET_WIDTH), BF16),
        jax.ShapeDtypeStruct(w_out.shape, BF16),
        jax.ShapeDtypeStruct(w_gate.shape, BF16),
        jax.ShapeDtypeStruct(w_up.shape, BF16),
        jax.ShapeDtypeStruct(w_down.shape, BF16),
    )
    weight_specs = [
        pl.BlockSpec((wo_rows, D_MODEL), row),
        pl.BlockSpec((wg_rows, d_ff), row),
        pl.BlockSpec((wg_rows, d_ff), row),
        pl.BlockSpec((wd_rows, D_MODEL), lambda i: (i // 2, 0)),
    ]
    out_specs = (
        pl.BlockSpec((DA_Q_COLS, tm), lambda i: (0, i)),
        pl.BlockSpec((DA_HEADS, tm, DA_QK_PAIR), lambda i: (0, i, 0)),
        pl.BlockSpec((tm // ATT_TILE, DA_V_COLS, ATT_TILE), lambda i: (i, 0, 0)),
        pl.BlockSpec((tm, RET_Q_COLS), row),
        pl.BlockSpec((tm, RET_K_COLS), row),
        pl.BlockSpec((tm, RET_WIDTH), row),
        pl.BlockSpec((tm, RET_WIDTH), row),
        *weight_specs,
    )
    return pl.pallas_call(
        _inproj_kernel,
        grid=(steps,),
        in_specs=[
            pl.BlockSpec((tm, D_MODEL), row),
            _const_spec((1, D_MODEL)),
            _const_spec(w_in.shape),
            *weight_specs,
        ],
        out_specs=out_specs,
        out_shape=out_shapes,
        scratch_shapes=[
            pltpu.VMEM((DA_Q_COLS, D_MODEL), BF16),
            pltpu.VMEM((D_MODEL, DA_K_COLS), BF16),
            pltpu.VMEM((DA_V_COLS, D_MODEL), BF16),
            pltpu.VMEM((D_MODEL, w_in.shape[1] - DA_Q_COLS - DA_K_COLS - DA_V_COLS), BF16),
        ],
        compiler_params=pltpu.CompilerParams(
            dimension_semantics=("arbitrary",), vmem_limit_bytes=VMEM_LIMIT_BYTES),
        name="inproj",
    )(x2, g, w_in, w_out, w_gate, w_up, w_down)


def _attn_tables():
    t = ATT_TILE
    kk = np.arange(t)[:, None]
    qq = np.arange(t)[None, :]
    allowed = (kk // CHUNK) <= (qq // CHUNK)
    diag = np.empty((DA_HEADS, t, 2 * t), np.float32)
    prow = np.zeros((DA_HEADS, ATT_POS_ROWS, 2 * ATT_QTILE), np.float32)
    for h, slope in enumerate(ALIBI_SLOPES):
        c = slope * LOG2E
        d = np.where(allowed, -2.0 * c * np.maximum(kk - qq, 0), -np.inf)
        diag[h] = np.concatenate([d, d], axis=1)
        rem = c
        for j in range(POS_TERMS):
            piece = float(np.asarray(rem, np.float32).astype(jnp.bfloat16).astype(np.float32))
            prow[h, j, :] = piece * CHUNK
            prow[h, POS_TERMS + j, :] = piece
            rem = rem - piece
    return diag, prow


def _pos_features(s):
    pos = np.arange(s)
    feat = np.zeros((s, ATT_POS_ROWS), np.float32)
    for j in range(POS_TERMS):
        feat[:, j] = pos // CHUNK
        feat[:, POS_TERMS + j] = pos % CHUNK
    return feat


def _attn_kernel(lam_init, qT_ref, k_ref, pos_ref, vT_ref, diag_ref, prow_ref, lamv_ref, g_ref,
                 o_ref, acc_ref, *sp_refs):
    t = ATT_TILE
    tq = ATT_QTILE
    w = 2 * tq
    nq = qT_ref.shape[1] // tq
    s_refs = sp_refs[:ATT_LOOKAHEAD + ATT_GROUP]
    p_refs = sp_refs[ATT_LOOKAHEAD + ATT_GROUP:]
    z = jnp.zeros((DA_QK_DIM, t), BF16)
    ones_rows = (lax.broadcasted_iota(jnp.int32, (ATT_DEN_ROWS, t), 0) == 0).astype(BF16)
    diag = diag_ref[0]
    lv = lamv_ref[...]
    lam = (jnp.exp(jnp.sum(lv[0:1] * lv[1:2], axis=-1, keepdims=True))
           - jnp.exp(jnp.sum(lv[2:3] * lv[3:4], axis=-1, keepdims=True)) + lam_init)

    def keys(k0, nk):
        rows = slice(k0 * t, (k0 + nk) * t)
        return jnp.concatenate([k_ref[rows, :], pos_ref[rows, :]], axis=1)

    def values_t(k0, nk):
        return jnp.concatenate(
            [jnp.concatenate([vT_ref[ki], ones_rows], axis=0) for ki in range(k0, k0 + nk)], axis=1)

    def colmax(a):
        return jnp.max(a, axis=0, keepdims=True)

    def prob(a, m):
        return jnp.exp2(a - m).astype(BF16)

    def combine(a):
        num = a[:DA_V_DIM] * (1.0 / a[DA_V_DIM:DA_V_DIM + 1])
        return num[:, :t] - lam * num[:, t:]

    def query_operand(qi):
        qh = qT_ref[:, qi * tq:(qi + 1) * tq]
        cols = [jnp.concatenate([qh[:DA_QK_DIM], z], axis=0),
                jnp.concatenate([z, qh[DA_QK_DIM:]], axis=0)]
        return jnp.concatenate([jnp.concatenate(cols, axis=1), prow_ref[0]], axis=0)

    tasks = []
    for qi in range(nq):
        tasks.append((qi, "diag", qi))
        tasks += [(qi, "full", k0) for k0 in range(qi)]

    qexts = {}

    def scores(task):
        qi, kind, k0 = task
        if qi not in qexts:
            qexts[qi] = query_operand(qi)
        s = jnp.dot(keys(k0, 1), qexts[qi], preferred_element_type=F32)
        return s + diag if kind == "diag" else s

    row0 = pl.multiple_of(jnp.minimum(pl.program_id(0), 0), t)

    def issue(idx):
        s = scores(tasks[idx])
        s_refs[idx % len(s_refs)][pl.ds(row0, t), :s.shape[1]] = s
        return colmax(s)

    def probabilities(gi, group, m):
        ncol = w
        p_ref = p_refs[gi % len(p_refs)]
        for j, idx in enumerate(group):
            s_ref = s_refs[idx % len(s_refs)]
            for r in range(0, t, ATT_EXP_ROWS):
                rows = pl.ds(pl.multiple_of(row0 + r, ATT_EXP_ROWS), ATT_EXP_ROWS)
                out_rows = pl.ds(pl.multiple_of(row0 + j * t + r, ATT_EXP_ROWS), ATT_EXP_ROWS)
                p_ref[out_rows, :ncol] = prob(s_ref[rows, :ncol], m)
        return p_ref[pl.ds(row0, len(group) * t), :ncol]

    groups = []
    idx = 0
    for qi in range(nq):
        groups.append([idx])
        idx += 1
        for g0 in range(0, qi, ATT_GROUP):
            size = min(ATT_GROUP, qi - g0)
            groups.append(list(range(idx, idx + size)))
            idx += size

    smaxes = []
    m = None
    for gi, group in enumerate(groups):
        while len(smaxes) < min(group[-1] + 1 + ATT_LOOKAHEAD, len(tasks)):
            smaxes.append(issue(len(smaxes)))
        qi, kind, k0 = tasks[group[0]]
        smax = functools.reduce(jnp.maximum, [smaxes[i] for i in group])
        acc = acc_ref.at[qi % 2]
        if kind == "diag":
            m = smax
            acc[...] = jnp.dot(
                values_t(k0, 1), probabilities(gi, group, m), preferred_element_type=F32)
        else:
            m_new = jnp.maximum(m, smax)
            pv = jnp.dot(values_t(k0, len(group)), probabilities(gi, group, m_new),
                         preferred_element_type=F32)
            acc[...] = acc[...] * jnp.exp2(m - m_new) + pv
            m = m_new
        if gi + 1 == len(groups) or tasks[groups[gi + 1][0]][0] != qi:
            o = combine(acc[...])
            o = o * lax.rsqrt(jnp.mean(o * o, axis=0, keepdims=True) + NORM_EPS)
            o_ref[qi * tq:(qi + 1) * tq, :] = (
                o.T * g_ref[...] * (1.0 - lam_init)).astype(o_ref.dtype)


def _attention(qT, k, vT, lamv, subln_g, batch, seq, lam_init):
    t = ATT_TILE
    tq = ATT_QTILE
    nk = seq // t
    n = batch * seq
    diag, prow = _attn_tables()
    diag = jnp.asarray(diag)
    prow = jnp.asarray(prow, BF16)
    pos = jnp.asarray(_pos_features(seq), BF16)
    return pl.pallas_call(
        functools.partial(_attn_kernel, lam_init),
        grid=(batch, DA_HEADS),
        in_specs=[
            pl.BlockSpec((DA_QK_PAIR, seq), lambda b, h: (h, b)),
            pl.BlockSpec((None, seq, DA_QK_PAIR), lambda b, h: (h, b, 0)),
            pl.BlockSpec((seq, ATT_POS_ROWS), lambda b, h: (0, 0)),
            pl.BlockSpec((nk, DA_V_DIM, t), lambda b, h: (b, h, 0)),
            pl.BlockSpec((1, t, 2 * t), lambda b, h: (h, 0, 0)),
            pl.BlockSpec((1, ATT_POS_ROWS, 2 * tq), lambda b, h: (h, 0, 0)),
            pl.BlockSpec((4, DA_QK_DIM), lambda b, h: (0, 0)),
            pl.BlockSpec((1, DA_V_DIM), lambda b, h: (0, 0)),
        ],
        out_specs=pl.BlockSpec((seq, DA_V_DIM), lambda b, h: (b, h)),
        out_shape=jax.ShapeDtypeStruct((n, DA_WIDTH), BF16),
        scratch_shapes=[
            pltpu.VMEM((2, DA_V_DIM + ATT_DEN_ROWS, 2 * tq), F32),
            *[pltpu.VMEM((t, 2 * tq), F32) for _ in range(ATT_LOOKAHEAD + ATT_GROUP)],
            *[pltpu.VMEM((ATT_GROUP * t, 2 * tq), BF16) for _ in range(ATT_LOOKAHEAD + 1)],
        ],
        compiler_params=pltpu.CompilerParams(
            dimension_semantics=("arbitrary", "arbitrary"),
            vmem_limit_bytes=VMEM_LIMIT_BYTES),
        name="diff_attention",
    )(qT, k, pos, vT, diag, prow, lamv, subln_g)


def _ret_tables():
    L = RET_BLOCK
    i = np.arange(L)[:, None].astype(np.float64)
    j = np.arange(L)[None, :].astype(np.float64)
    allowed = (j // CHUNK) <= (i // CHUNK)
    dmat = np.empty((RET_HEADS, L, L), np.float32)
    lanes = np.zeros((RET_HEADS, 1, RET_QK_PAIR), np.float32)
    qdec = np.empty((RET_HEADS, L, RET_V_DIM), np.float32)
    kdec = np.zeros((RET_HEADS, L, RET_QK_PAIR), np.float32)
    cdec = np.empty((RET_HEADS, 1, RET_V_DIM), np.float32)
    lane = np.arange(RET_QK_PAIR)[None, :]
    for h, gamma in enumerate(RET_GAMMA):
        mine = (lane // RET_QK_DIM) == (h % 2)
        dmat[h] = np.where(allowed, gamma ** np.abs(i - j), 0.0) * RET_QK_DIM ** -0.5
        lanes[h] = mine
        qdec[h] = gamma ** (i + 1.0)
        kdec[h] = np.where(mine, gamma ** (L - 1.0 - i), 0.0) * RET_QK_DIM ** -0.5
        cdec[h] = gamma ** L
    return dmat, lanes, qdec, kdec, cdec


def _ret_kernel(q_ref, k_ref, v_ref, gate_ref, dmat_ref, lanes_ref, qdec_ref, kdec_ref, cdec_ref,
                g_ref, o_ref):
    L = RET_BLOCK
    tasks = [(blk, h) for blk in range(q_ref.shape[0] // L) for h in range(RET_HEADS)]
    own_lanes = [jnp.broadcast_to(lanes_ref[h] > 0.0, (L, RET_QK_PAIR)) for h in range(RET_HEADS)]

    def operands(task):
        blk, h = task
        rows = slice(blk * L, (blk + 1) * L)
        pair = slice((h // 2) * RET_QK_PAIR, (h // 2 + 1) * RET_QK_PAIR)
        return rows, slice(h * RET_V_DIM, (h + 1) * RET_V_DIM), q_ref[rows, pair], k_ref[rows, pair]

    def scores(task):
        _, h = task
        _, _, q, k = operands(task)
        qm = jnp.where(own_lanes[h], q, jnp.zeros_like(q))
        s = lax.dot_general(qm, k, _NT, preferred_element_type=F32) * dmat_ref[h]
        return s.astype(BF16)

    states = [jnp.zeros((RET_QK_PAIR, RET_V_DIM), F32) for _ in range(RET_HEADS)]
    pending = [scores(task) for task in tasks[:RET_LOOKAHEAD]]
    for idx, task in enumerate(tasks):
        _, h = task
        s = pending.pop(0)
        if idx + RET_LOOKAHEAD < len(tasks):
            pending.append(scores(tasks[idx + RET_LOOKAHEAD]))
        rows, head, q, k = operands(task)
        v = v_ref[rows, head]
        inner = jnp.dot(s, v, preferred_element_type=F32)
        cross = jnp.dot(q, states[h].astype(BF16), preferred_element_type=F32) * qdec_ref[h]
        kd = (k.astype(F32) * kdec_ref[h]).astype(BF16)
        states[h] = states[h] * cdec_ref[h] + lax.dot_general(
            kd, v, _TN, preferred_element_type=F32)
        o = _rms(inner + cross) * g_ref[...]
        o_ref[rows, head] = (_silu(gate_ref[rows, head].astype(F32)) * o).astype(o_ref.dtype)


def _retention(rq, rk, rv, rg, norm_g, batch, seq):
    n = batch * seq
    tables = [jnp.asarray(a) for a in _ret_tables()]
    row = lambda b: (b, 0)
    return pl.pallas_call(
        _ret_kernel,
        grid=(batch,),
        in_specs=[
            pl.BlockSpec((seq, RET_Q_COLS), row),
            pl.BlockSpec((seq, RET_K_COLS), row),
            pl.BlockSpec((seq, RET_WIDTH), row),
            pl.BlockSpec((seq, RET_WIDTH), row),
            *[_const_spec(a.shape) for a in tables],
            _const_spec((1, RET_V_DIM)),
        ],
        out_specs=pl.BlockSpec((seq, RET_WIDTH), row),
        out_shape=jax.ShapeDtypeStruct((n, RET_WIDTH), BF16),
        compiler_params=pltpu.CompilerParams(
            dimension_semantics=("arbitrary",), vmem_limit_bytes=VMEM_LIMIT_BYTES),
        name="retention",
    )(rq, rk, rv, rg, *tables, norm_g)


def _ffn_kernel(x_ref, oa_ref, or_ref, wo_ref, g1_ref, g2_ref, wg_ref, wu_ref, wd_ref, g3_ref,
                out_ref):
    subs = [slice(j * FFN_SUB, (j + 1) * FFN_SUB) for j in range(x_ref.shape[0] // FFN_SUB)]
    mix = [jnp.dot(oa_ref[r, :], wo_ref[:DA_WIDTH, :], preferred_element_type=F32)
           + jnp.dot(or_ref[r, :], wo_ref[DA_WIDTH:, :], preferred_element_type=F32) for r in subs]
    x1 = [x_ref[r, :] + _rms(mx) * g1_ref[...] for r, mx in zip(subs, mix)]
    h = [(_rms(a) * g2_ref[...]).astype(BF16) for a in x1]
    f = []
    y = []
    for a in h:
        gt = jnp.dot(a, wg_ref[...], preferred_element_type=F32)
        u = jnp.dot(a, wu_ref[...], preferred_element_type=F32)
        if f:
            y.append(jnp.dot(f[-1], wd_ref[...], preferred_element_type=F32))
        f.append((_silu(gt) * u).astype(BF16))
    y.append(jnp.dot(f[-1], wd_ref[...], preferred_element_type=F32))
    for r, a, b in zip(subs, x1, y):
        out_ref[r, :] = a + _rms(b) * g3_ref[...]


def _ffn(x2, oa, orr, wo, g1, g2, wg, wu, wd, g3):
    n = x2.shape[0]
    tm = ROW_TILE
    row = lambda i: (i, 0)
    return pl.pallas_call(
        _ffn_kernel,
        grid=(n // tm,),
        in_specs=[
            pl.BlockSpec((tm, D_MODEL), row),
            pl.BlockSpec((tm, DA_WIDTH), row),
            pl.BlockSpec((tm, RET_WIDTH), row),
            _const_spec(wo.shape),
            _const_spec((1, D_MODEL)),
            _const_spec((1, D_MODEL)),
            _const_spec(wg.shape),
            _const_spec(wu.shape),
            _const_spec(wd.shape),
            _const_spec((1, D_MODEL)),
        ],
        out_specs=pl.BlockSpec((tm, D_MODEL), row),
        out_shape=jax.ShapeDtypeStruct((n, D_MODEL), F32),
        compiler_params=pltpu.CompilerParams(
            dimension_semantics=("arbitrary",), vmem_limit_bytes=VMEM_LIMIT_BYTES),
        name="outproj_ffn",
    )(x2, oa, orr, wo, g1, g2, wg, wu, wd, g3)


def kernel(x, pre_mix_g, w_in, lambda_q1, lambda_k1, lambda_q2, lambda_k2, da_subln_g, ret_norm_g,
           w_out, post_mix_g, pre_ffn_g, w_gate, w_up, w_down, post_ffn_g):
    batch, seq, d = x.shape
    depth = w_in.shape[0]
    assert d == D_MODEL and seq % ATT_QTILE == 0 and seq % RET_BLOCK == 0
    assert ATT_QTILE == ATT_TILE and ATT_TILE % CHUNK == 0 and RET_BLOCK % CHUNK == 0
    assert DA_QK_PAIR == V7X_LANES and RET_QK_PAIR == V7X_LANES
    assert DA_V_DIM == V7X_LANES and RET_V_DIM == V7X_LANES and ATT_TILE == V7X_MXU_DIM
    assert (batch * seq) % ROW_TILE == 0 and ROW_TILE % FFN_SUB == 0
    assert (batch * seq) % INPROJ_TILE == 0 and INPROJ_TILE % ATT_TILE == 0
    x2 = x.reshape(batch * seq, d)
    for l in range(depth):
        qT, k, vT, rq, rk, rv, rg, wo, wg, wu, wd = _inproj(
            x2, pre_mix_g[l][None, :], w_in[l], w_out[l], w_gate[l], w_up[l], w_down[l])
        lam_init = 0.8 - 0.6 * math.exp(-0.3 * l)
        lamv = jnp.stack([lambda_q1[l], lambda_k1[l], lambda_q2[l], lambda_k2[l]]).astype(F32)
        o_da = _attention(qT, k, vT, lamv, da_subln_g[l][None, :].astype(F32), batch, seq, lam_init)
        o_ret = _retention(rq, rk, rv, rg, ret_norm_g[l][None, :].astype(F32), batch, seq)
        x2 = _ffn(x2, o_da, o_ret, wo, post_mix_g[l][None, :], pre_ffn_g[l][None, :],
                  wg, wu, wd, post_ffn_g[l][None, :])
    return x2.reshape(batch, seq, d)
```

```python
import functools
import math

import numpy as np
import jax
import jax.numpy as jnp
from jax import lax
from jax.experimental import pallas as pl
from jax.experimental.pallas import tpu as pltpu

F32 = jnp.float32
BF16 = jnp.bfloat16

D_MODEL = 1024
CHUNK = 64
NORM_EPS = 1e-6
DA_HEADS = 4
DA_QK_DIM = 64
DA_V_DIM = 128
DA_WIDTH = DA_HEADS * DA_V_DIM
ALIBI_SLOPES = tuple(2.0 ** (-8.0 * (h + 1) / DA_HEADS) for h in range(DA_HEADS))
RET_HEADS = 4
RET_QK_DIM = 64
RET_V_DIM = 128
RET_WIDTH = RET_HEADS * RET_V_DIM
RET_GAMMA = tuple(1.0 - 2.0 ** (-5.0 - h) for h in range(RET_HEADS))
DA_Q_COLS = DA_HEADS * 2 * DA_QK_DIM
DA_K_COLS = DA_HEADS * 2 * DA_QK_DIM
DA_V_COLS = DA_WIDTH
RET_Q_COLS = RET_HEADS * RET_QK_DIM
RET_K_COLS = RET_HEADS * RET_QK_DIM
DA_QK_PAIR = 2 * DA_QK_DIM
RET_QK_PAIR = 2 * RET_QK_DIM

V7X_LANES = 128
V7X_BF16_SUBLANES = 16
V7X_MXU_DIM = 256
V7X_VMEM_BYTES = 64 * 1024 * 1024
VMEM_LIMIT_BYTES = V7X_VMEM_BYTES - 8 * 1024 * 1024

INPROJ_TILE = 1024
ROW_TILE = 1024
FFN_SUB = 256
FFN_CHUNK = 1024
ATT_TILE = 256
ATT_QTILE = 256
ATT_DEN_ROWS = 16
ATT_GROUP = 1
ATT_LOOKAHEAD = 5
ATT_EXP_ROWS = 32
ATT_POS_ROWS = V7X_MXU_DIM - DA_QK_PAIR
RET_BLOCK = 256
RET_LOOKAHEAD = 2
LOG2E = math.log2(math.e)
POS_TERMS = 3

_NT = (((1,), (1,)), ((), ()))
_TN = (((0,), (0,)), ((), ()))


def _rms(x):
    return x * lax.rsqrt(jnp.mean(x * x, axis=-1, keepdims=True) + NORM_EPS)


def _silu(x):
    hx = 0.5 * x
    return hx * (jnp.tanh(hx) + 1.0)


def _const_spec(shape):
    nd = len(shape)
    return pl.BlockSpec(shape, lambda *_: (0,) * nd, pipeline_mode=pl.Buffered(1))


def _inproj_kernel(x_ref, g_ref, win_ref, wo32_ref, wg32_ref, wu32_ref, wd32_ref,
                   qT_ref, k_ref, vT_ref, rq_ref, rk_ref, rv_ref, rg_ref,
                   wo_ref, wg_ref, wu_ref, wd_ref,
                   wqT_ref, wk_ref, wvT_ref, wr_ref):
    c1 = DA_Q_COLS
    c2 = c1 + DA_K_COLS
    c3 = c2 + DA_V_COLS

    @pl.when(pl.program_id(0) == 0)
    def _():
        for c in range(0, DA_Q_COLS, V7X_LANES):
            cols = slice(c, c + V7X_LANES)
            wqT_ref[cols, :] = win_ref[:, cols].T.astype(BF16)
            wvT_ref[cols, :] = win_ref[:, c2 + c:c2 + c + V7X_LANES].T.astype(BF16)
        wk_ref[...] = win_ref[:, c1:c2].astype(BF16)
        wr_ref[...] = win_ref[:, c3:].astype(BF16)

    wo_ref[...] = wo32_ref[...].astype(BF16)
    wg_ref[...] = wg32_ref[...].astype(BF16)
    wu_ref[...] = wu32_ref[...].astype(BF16)
    wd_ref[...] = wd32_ref[...].astype(BF16)

    for j in range(INPROJ_TILE // ATT_TILE):
        rows = slice(j * ATT_TILE, (j + 1) * ATT_TILE)
        h = (_rms(x_ref[rows, :]) * g_ref[...]).astype(BF16)
        qT = lax.dot_general(wqT_ref[...], h, _NT, preferred_element_type=F32)
        qT_ref[:, rows] = (qT * (DA_QK_DIM ** -0.5 * LOG2E)).astype(BF16)
        k = jnp.dot(h, wk_ref[...], preferred_element_type=F32)
        for hh in range(DA_HEADS):
            k_ref[hh, rows, :] = k[:, hh * DA_QK_PAIR:(hh + 1) * DA_QK_PAIR].astype(BF16)
        vT = lax.dot_general(wvT_ref[...], h, _NT, preferred_element_type=F32)
        vT_ref[j] = vT.astype(BF16)
        r = jnp.dot(h, wr_ref[...], preferred_element_type=F32)
        r1 = RET_Q_COLS
        r2 = r1 + RET_K_COLS
        r3 = r2 + RET_WIDTH
        rq_ref[rows, :] = r[:, :r1].astype(BF16)
        rk_ref[rows, :] = r[:, r1:r2].astype(BF16)
        rv_ref[rows, :] = r[:, r2:r3].astype(BF16)
        rg_ref[rows, :] = r[:, r3:].astype(BF16)


def _inproj(x2, g, w_in, w_out, w_gate, w_up, w_down):
    n = x2.shape[0]
    tm = INPROJ_TILE
    steps = n // tm
    d_ff = w_gate.shape[1]
    wo_rows, wg_rows, wd_rows = D_MODEL // steps, D_MODEL // steps, d_ff // (steps // 2)
    assert wo_rows * steps == D_MODEL and wd_rows * (steps // 2) == d_ff
    assert wo_rows % V7X_BF16_SUBLANES == 0 and wd_rows % V7X_BF16_SUBLANES == 0
    row = lambda i: (i, 0)
    out_shapes = (
        jax.ShapeDtypeStruct((DA_Q_COLS, n), BF16),
        jax.ShapeDtypeStruct((DA_HEADS, n, DA_QK_PAIR), BF16),
        jax.ShapeDtypeStruct((n // ATT_TILE, DA_V_COLS, ATT_TILE), BF16),
        jax.ShapeDtypeStruct((n, RET_Q_COLS), BF16),
        jax.ShapeDtypeStruct((n, RET_K_COLS), BF16),
        jax.ShapeDtypeStruct((n, RET_WIDTH), BF16),
        jax.ShapeDtypeStruct((n, RET_WIDTH), BF16),
        jax.ShapeDtypeStruct(w_out.shape, BF16),
        jax.ShapeDtypeStruct(w_gate.shape, BF16),
        jax.ShapeDtypeStruct(w_up.shape, BF16),
        jax.ShapeDtypeStruct(w_down.shape, BF16),
    )
    weight_specs = [
        pl.BlockSpec((wo_rows, D_MODEL), row),
        pl.BlockSpec((wg_rows, d_ff), row),
        pl.BlockSpec((wg_rows, d_ff), row),
        pl.BlockSpec((wd_rows, D_MODEL), lambda i: (i // 2, 0)),
    ]
    out_specs = (
        pl.BlockSpec((DA_Q_COLS, tm), lambda i: (0, i)),
        pl.BlockSpec((DA_HEADS, tm, DA_QK_PAIR), lambda i: (0, i, 0)),
        pl.BlockSpec((tm // ATT_TILE, DA_V_COLS, ATT_TILE), lambda i: (i, 0, 0)),
        pl.BlockSpec((tm, RET_Q_COLS), row),
        pl.BlockSpec((tm, RET_K_COLS), row),
        pl.BlockSpec((tm, RET_WIDTH), row),
        pl.BlockSpec((tm, RET_WIDTH), row),
        *weight_specs,
    )
    return pl.pallas_call(
        _inproj_kernel,
        grid=(steps,),
        in_specs=[
            pl.BlockSpec((tm, D_MODEL), row),
            _const_spec((1, D_MODEL)),
            _const_spec(w_in.shape),
            *weight_specs,
        ],
        out_specs=out_specs,
        out_shape=out_shapes,
        scratch_shapes=[
            pltpu.VMEM((DA_Q_COLS, D_MODEL), BF16),
            pltpu.VMEM((D_MODEL, DA_K_COLS), BF16),
            pltpu.VMEM((DA_V_COLS, D_MODEL), BF16),
            pltpu.VMEM((D_MODEL, w_in.shape[1] - DA_Q_COLS - DA_K_COLS - DA_V_COLS), BF16),
        ],
        compiler_params=pltpu.CompilerParams(
            dimension_semantics=("arbitrary",), vmem_limit_bytes=VMEM_LIMIT_BYTES),
        name="inproj",
    )(x2, g, w_in, w_out, w_gate, w_up, w_down)


def _attn_tables():
    t = ATT_TILE
    kk = np.arange(t)[:, None]
    qq = np.arange(t)[None, :]
    allowed = (kk // CHUNK) <= (qq // CHUNK)
    diag = np.empty((DA_HEADS, t, 2 * t), np.float32)
    prow = np.zeros((DA_HEADS, ATT_POS_ROWS, 2 * ATT_QTILE), np.float32)
    for h, slope in enumerate(ALIBI_SLOPES):
        c = slope * LOG2E
        d = np.where(allowed, -2.0 * c * np.maximum(kk - qq, 0), -np.inf)
        diag[h] = np.concatenate([d, d], axis=1)
        rem = c
        for j in range(POS_TERMS):
            piece = float(np.asarray(rem, np.float32).astype(jnp.bfloat16).astype(np.float32))
            prow[h, j, :] = piece * CHUNK
            prow[h, POS_TERMS + j, :] = piece
            rem = rem - piece
    return diag, prow


def _pos_features(s):
    pos = np.arange(s)
    feat = np.zeros((s, ATT_POS_ROWS), np.float32)
    for j in range(POS_TERMS):
        feat[:, j] = pos // CHUNK
        feat[:, POS_TERMS + j] = pos % CHUNK
    return feat


def _attn_kernel(lam_init, qT_ref, k_ref, pos_ref, vT_ref, diag_ref, prow_ref, lamv_ref, g_ref,
                 o_ref, acc_ref, *sp_refs):
    t = ATT_TILE
    tq = ATT_QTILE
    w = 2 * tq
    nq = qT_ref.shape[1] // tq
    s_refs = sp_refs[:ATT_LOOKAHEAD + ATT_GROUP]
    p_refs = sp_refs[ATT_LOOKAHEAD + ATT_GROUP:]
    z = jnp.zeros((DA_QK_DIM, t), BF16)
    ones_rows = (lax.broadcasted_iota(jnp.int32, (ATT_DEN_ROWS, t), 0) == 0).astype(BF16)
    diag = diag_ref[0]
    lv = lamv_ref[...]
    lam = (jnp.exp(jnp.sum(lv[0:1] * lv[1:2], axis=-1, keepdims=True))
           - jnp.exp(jnp.sum(lv[2:3] * lv[3:4], axis=-1, keepdims=True)) + lam_init)

    def keys(k0, nk):
        rows = slice(k0 * t, (k0 + nk) * t)
        return jnp.concatenate([k_ref[rows, :], pos_ref[rows, :]], axis=1)

    def values_t(k0, nk):
        return jnp.concatenate(
            [jnp.concatenate([vT_ref[ki], ones_rows], axis=0) for ki in range(k0, k0 + nk)], axis=1)

    def colmax(a):
        return jnp.max(a, axis=0, keepdims=True)

    def prob(a, m):
        return jnp.exp2(a - m).astype(BF16)

    def combine(a):
        num = a[:DA_V_DIM] * (1.0 / a[DA_V_DIM:DA_V_DIM + 1])
        return num[:, :t] - lam * num[:, t:]

    def query_operand(qi):
        qh = qT_ref[:, qi * tq:(qi + 1) * tq]
        cols = [jnp.concatenate([qh[:DA_QK_DIM], z], axis=0),
                jnp.concatenate([z, qh[DA_QK_DIM:]], axis=0)]
        return jnp.concatenate([jnp.concatenate(cols, axis=1), prow_ref[0]], axis=0)

    tasks = []
    for qi in range(nq):
        tasks.append((qi, "diag", qi))
        tasks += [(qi, "full", k0) for k0 in range(qi)]

    qexts = {}

    def scores(task):
        qi, kind, k0 = task
        if qi not in qexts:
            qexts[qi] = query_operand(qi)
        s = jnp.dot(keys(k0, 1), qexts[qi], preferred_element_type=F32)
        return s + diag if kind == "diag" else s

    row0 = pl.multiple_of(jnp.minimum(pl.program_id(0), 0), t)

    def issue(idx):
        s = scores(tasks[idx])
        s_refs[idx % len(s_refs)][pl.ds(row0, t), :s.shape[1]] = s
        return colmax(s)

    def probabilities(gi, group, m):
        ncol = w
        p_ref = p_refs[gi % len(p_refs)]
        for j, idx in enumerate(group):
            s_ref = s_refs[idx % len(s_refs)]
            for r in range(0, t, ATT_EXP_ROWS):
                rows = pl.ds(pl.multiple_of(row0 + r, ATT_EXP_ROWS), ATT_EXP_ROWS)
                out_rows = pl.ds(pl.multiple_of(row0 + j * t + r, ATT_EXP_ROWS), ATT_EXP_ROWS)
                p_ref[out_rows, :ncol] = prob(s_ref[rows, :ncol], m)
        return p_ref[pl.ds(row0, len(group) * t), :ncol]

    groups = []
    idx = 0
    for qi in range(nq):
        groups.append([idx])
        idx += 1
        for g0 in range(0, qi, ATT_GROUP):
            size = min(ATT_GROUP, qi - g0)
            groups.append(list(range(idx, idx + size)))
            idx += size

    smaxes = []
    m = None
    for gi, group in enumerate(groups):
        while len(smaxes) < min(group[-1] + 1 + ATT_LOOKAHEAD, len(tasks)):
            smaxes.append(issue(len(smaxes)))
        qi, kind, k0 = tasks[group[0]]
        smax = functools.reduce(jnp.maximum, [smaxes[i] for i in group])
        acc = acc_ref.at[qi % 2]
        if kind == "diag":
            m = smax
            acc[...] = jnp.dot(
                values_t(k0, 1), probabilities(gi, group, m), preferred_element_type=F32)
        else:
            m_new = jnp.maximum(m, smax)
            pv = jnp.dot(values_t(k0, len(group)), probabilities(gi, group, m_new),
                         preferred_element_type=F32)
            acc[...] = acc[...] * jnp.exp2(m - m_new) + pv
            m = m_new
        if gi + 1 == len(groups) or tasks[groups[gi + 1][0]][0] != qi:
            o = combine(acc[...])
            o = o * lax.rsqrt(jnp.mean(o * o, axis=0, keepdims=True) + NORM_EPS)
            o_ref[qi * tq:(qi + 1) * tq, :] = (
                o.T * g_ref[...] * (1.0 - lam_init)).astype(o_ref.dtype)


def _attention(qT, k, vT, lamv, subln_g, batch, seq, lam_init):
    t = ATT_TILE
    tq = ATT_QTILE
    nk = seq // t
    n = batch * seq
    diag, prow = _attn_tables()
    diag = jnp.asarray(diag)
    prow = jnp.asarray(prow, BF16)
    pos = jnp.asarray(_pos_features(seq), BF16)
    return pl.pallas_call(
        functools.partial(_attn_kernel, lam_init),
        grid=(batch, DA_HEADS),
        in_specs=[
            pl.BlockSpec((DA_QK_PAIR, seq), lambda b, h: (h, b)),
            pl.BlockSpec((None, seq, DA_QK_PAIR), lambda b, h: (h, b, 0)),
            pl.BlockSpec((seq, ATT_POS_ROWS), lambda b, h: (0, 0)),
            pl.BlockSpec((nk, DA_V_DIM, t), lambda b, h: (b, h, 0)),
            pl.BlockSpec((1, t, 2 * t), lambda b, h: (h, 0, 0)),
            pl.BlockSpec((1, ATT_POS_ROWS, 2 * tq), lambda b, h: (h, 0, 0)),
            pl.BlockSpec((4, DA_QK_DIM), lambda b, h: (0, 0)),
            pl.BlockSpec((1, DA_V_DIM), lambda b, h: (0, 0)),
        ],
        out_specs=pl.BlockSpec((seq, DA_V_DIM), lambda b, h: (b, h)),
        out_shape=jax.ShapeDtypeStruct((n, DA_WIDTH), BF16),
        scratch_shapes=[
            pltpu.VMEM((2, DA_V_DIM + ATT_DEN_ROWS, 2 * tq), F32),
            *[pltpu.VMEM((t, 2 * tq), F32) for _ in range(ATT_LOOKAHEAD + ATT_GROUP)],
            *[pltpu.VMEM((ATT_GROUP * t, 2 * tq), BF16) for _ in range(ATT_LOOKAHEAD + 1)],
        ],
        compiler_params=pltpu.CompilerParams(
            dimension_semantics=("arbitrary", "arbitrary"),
            vmem_limit_bytes=VMEM_LIMIT_BYTES),
        name="diff_attention",
    )(qT, k, pos, vT, diag, prow, lamv, subln_g)


def _ret_tables():
    L = RET_BLOCK
    i = np.arange(L)[:, None].astype(np.float64)
    j = np.arange(L)[None, :].astype(np.float64)
    allowed = (j // CHUNK) <= (i // CHUNK)
    dmat = np.empty((RET_HEADS, L, L), np.float32)
    lanes = np.zeros((RET_HEADS, 1, RET_QK_PAIR), np.float32)
    qdec = np.empty((RET_HEADS, L, RET_V_DIM), np.float32)
    kdec = np.zeros((RET_HEADS, L, RET_QK_PAIR), np.float32)
    cdec = np.empty((RET_HEADS, 1, RET_V_DIM), np.float32)
    lane = np.arange(RET_QK_PAIR)[None, :]
    for h, gamma in enumerate(RET_GAMMA):
        mine = (lane // RET_QK_DIM) == (h % 2)
        dmat[h] = np.where(allowed, gamma ** np.abs(i - j), 0.0) * RET_QK_DIM ** -0.5
        lanes[h] = mine
        qdec[h] = gamma ** (i + 1.0)
        kdec[h] = np.where(mine, gamma ** (L - 1.0 - i), 0.0) * RET_QK_DIM ** -0.5
        cdec[h] = gamma ** L
    return dmat, lanes, qdec, kdec, cdec


def _ret_kernel(q_ref, k_ref, v_ref, gate_ref, dmat_ref, lanes_ref, qdec_ref, kdec_ref, cdec_ref,
                g_ref, o_ref):
    L = RET_BLOCK
    tasks = [(blk, h) for blk in range(q_ref.shape[0] // L) for h in range(RET_HEADS)]
    own_lanes = [jnp.broadcast_to(lanes_ref[h] > 0.0, (L, RET_QK_PAIR)) for h in range(RET_HEADS)]

    def operands(task):
        blk, h = task
        rows = slice(blk * L, (blk + 1) * L)
        pair = slice((h // 2) * RET_QK_PAIR, (h // 2 + 1) * RET_QK_PAIR)
        return rows, slice(h * RET_V_DIM, (h + 1) * RET_V_DIM), q_ref[rows, pair], k_ref[rows, pair]

    def scores(task):
        _, h = task
        _, _, q, k = operands(task)
        qm = jnp.where(own_lanes[h], q, jnp.zeros_like(q))
        s = lax.dot_general(qm, k, _NT, preferred_element_type=F32) * dmat_ref[h]
        return s.astype(BF16)

    states = [jnp.zeros((RET_QK_PAIR, RET_V_DIM), F32) for _ in range(RET_HEADS)]
    pending = [scores(task) for task in tasks[:RET_LOOKAHEAD]]
    for idx, task in enumerate(tasks):
        _, h = task
        s = pending.pop(0)
        if idx + RET_LOOKAHEAD < len(tasks):
            pending.append(scores(tasks[idx + RET_LOOKAHEAD]))
        rows, head, q, k = operands(task)
        v = v_ref[rows, head]
        inner = jnp.dot(s, v, preferred_element_type=F32)
        cross = jnp.dot(q, states[h].astype(BF16), preferred_element_type=F32) * qdec_ref[h]
        kd = (k.astype(F32) * kdec_ref[h]).astype(BF16)
        states[h] = states[h] * cdec_ref[h] + lax.dot_general(
            kd, v, _TN, preferred_element_type=F32)
        o = _rms(inner + cross) * g_ref[...]
        o_ref[rows, head] = (_silu(gate_ref[rows, head].astype(F32)) * o).astype(o_ref.dtype)


def _retention(rq, rk, rv, rg, norm_g, batch, seq):
    n = batch * seq
    tables = [jnp.asarray(a) for a in _ret_tables()]
    row = lambda b: (b, 0)
    return pl.pallas_call(
        _ret_kernel,
        grid=(batch,),
        in_specs=[
            pl.BlockSpec((seq, RET_Q_COLS), row),
            pl.BlockSpec((seq, RET_K_COLS), row),
            pl.BlockSpec((seq, RET_WIDTH), row),
            pl.BlockSpec((seq, RET_WIDTH), row),
            *[_const_spec(a.shape) for a in tables],
            _const_spec((1, RET_V_DIM)),
        ],
        out_specs=pl.BlockSpec((seq, RET_WIDTH), row),
        out_shape=jax.ShapeDtypeStruct((n, RET_WIDTH), BF16),
        compiler_params=pltpu.CompilerParams(
            dimension_semantics=("arbitrary",), vmem_limit_bytes=VMEM_LIMIT_BYTES),
        name="retention",
    )(rq, rk, rv, rg, *tables, norm_g)


def _ffn_kernel(x_ref, oa_ref, or_ref, wo_ref, g1_ref, g2_ref, wg_ref, wu_ref, wd_ref, g3_ref,
                out_ref):
    subs = [slice(j * FFN_SUB, (j + 1) * FFN_SUB) for j in range(x_ref.shape[0] // FFN_SUB)]
    mix = [jnp.dot(oa_ref[r, :], wo_ref[:DA_WIDTH, :], preferred_element_type=F32)
           + jnp.dot(or_ref[r, :], wo_ref[DA_WIDTH:, :], preferred_element_type=F32) for r in subs]
    x1 = [x_ref[r, :] + _rms(mx) * g1_ref[...] for r, mx in zip(subs, mix)]
    h = [(_rms(a) * g2_ref[...]).astype(BF16) for a in x1]
    d_ff = wg_ref.shape[1]
    chunks = [slice(c, min(c + FFN_CHUNK, d_ff)) for c in range(0, d_ff, FFN_CHUNK)]
    y = []
    for a in h:
        acc = None
        prev = None
        for cols in chunks:
            gt = jnp.dot(a, wg_ref[:, cols], preferred_element_type=F32)
            u = jnp.dot(a, wu_ref[:, cols], preferred_element_type=F32)
            if prev is not None:
                part = jnp.dot(prev[0], wd_ref[prev[1], :], preferred_element_type=F32)
                acc = part if acc is None else acc + part
            prev = ((_silu(gt) * u).astype(BF16), cols)
        part = jnp.dot(prev[0], wd_ref[prev[1], :], preferred_element_type=F32)
        y.append(part if acc is None else acc + part)
    for r, a, b in zip(subs, x1, y):
        out_ref[r, :] = a + _rms(b) * g3_ref[...]


def _ffn(x2, oa, orr, wo, g1, g2, wg, wu, wd, g3):
    n = x2.shape[0]
    tm = ROW_TILE
    row = lambda i: (i, 0)
    return pl.pallas_call(
        _ffn_kernel,
        grid=(n // tm,),
        in_specs=[
            pl.BlockSpec((tm, D_MODEL), row),
            pl.BlockSpec((tm, DA_WIDTH), row),
            pl.BlockSpec((tm, RET_WIDTH), row),
            _const_spec(wo.shape),
            _const_spec((1, D_MODEL)),
            _const_spec((1, D_MODEL)),
            _const_spec(wg.shape),
            _const_spec(wu.shape),
            _const_spec(wd.shape),
            _const_spec((1, D_MODEL)),
        ],
        out_specs=pl.BlockSpec((tm, D_MODEL), row),
        out_shape=jax.ShapeDtypeStruct((n, D_MODEL), F32),
        compiler_params=pltpu.CompilerParams(
            dimension_semantics=("arbitrary",), vmem_limit_bytes=VMEM_LIMIT_BYTES),
        name="outproj_ffn",
    )(x2, oa, orr, wo, g1, g2, wg, wu, wd, g3)


def kernel(x, pre_mix_g, w_in, lambda_q1, lambda_k1, lambda_q2, lambda_k2, da_subln_g, ret_norm_g,
           w_out, post_mix_g, pre_ffn_g, w_gate, w_up, w_down, post_ffn_g):
    batch, seq, d = x.shape
    depth = w_in.shape[0]
    assert d == D_MODEL and seq % ATT_QTILE == 0 and seq % RET_BLOCK == 0
    assert ATT_QTILE == ATT_TILE and ATT_TILE % CHUNK == 0 and RET_BLOCK % CHUNK == 0
    assert DA_QK_PAIR == V7X_LANES and RET_QK_PAIR == V7X_LANES
    assert DA_V_DIM == V7X_LANES and RET_V_DIM == V7X_LANES and ATT_TILE == V7X_MXU_DIM
    assert (batch * seq) % ROW_TILE == 0 and ROW_TILE % FFN_SUB == 0
    assert (batch * seq) % INPROJ_TILE == 0 and INPROJ_TILE % ATT_TILE == 0
    x2 = x.reshape(batch * seq, d)
    for l in range(depth):
        qT, k, vT, rq, rk, rv, rg, wo, wg, wu, wd = _inproj(
            x2, pre_mix_g[l][None, :], w_in[l], w_out[l], w_gate[l], w_up[l], w_down[l])
        lam_init = 0.8 - 0.6 * math.exp(-0.3 * l)
        lamv = jnp.stack([lambda_q1[l], lambda_k1[l], lambda_q2[l], lambda_k2[l]]).astype(F32)
        o_da = _attention(qT, k, vT, lamv, da_subln_g[l][None, :].astype(F32), batch, seq, lam_init)
        o_ret = _retention(rq, rk, rv, rg, ret_norm_g[l][None, :].astype(F32), batch, seq)
        x2 = _ffn(x2, o_da, o_ret, wo, post_mix_g[l][None, :], pre_ffn_g[l][None, :],
                  wg, wu, wd, post_ffn_g[l][None, :])
    return x2.reshape(batch, seq, d)
```

```python
import functools
import math

import numpy as np
import jax
import jax.numpy as jnp
from jax import lax
from jax.experimental import pallas as pl
from jax.experimental.pallas import tpu as pltpu

F32 = jnp.float32
BF16 = jnp.bfloat16

D_MODEL = 1024
CHUNK = 64
NORM_EPS = 1e-6
DA_HEADS = 4
DA_QK_DIM = 64
DA_V_DIM = 128
DA_WIDTH = DA_HEADS * DA_V_DIM
ALIBI_SLOPES = tuple(2.0 ** (-8.0 * (h + 1) / DA_HEADS) for h in range(DA_HEADS))
RET_HEADS = 4
RET_QK_DIM = 64
RET_V_DIM = 128
RET_WIDTH = RET_HEADS * RET_V_DIM
RET_GAMMA = tuple(1.0 - 2.0 ** (-5.0 - h) for h in range(RET_HEADS))
DA_Q_COLS = DA_HEADS * 2 * DA_QK_DIM
DA_K_COLS = DA_HEADS * 2 * DA_QK_DIM
DA_V_COLS = DA_WIDTH
RET_Q_COLS = RET_HEADS * RET_QK_DIM
RET_K_COLS = RET_HEADS * RET_QK_DIM
DA_QK_PAIR = 2 * DA_QK_DIM
RET_QK_PAIR = 2 * RET_QK_DIM

V7X_LANES = 128
V7X_BF16_SUBLANES = 16
V7X_MXU_DIM = 256
V7X_VMEM_BYTES = 64 * 1024 * 1024
VMEM_LIMIT_BYTES = V7X_VMEM_BYTES - 8 * 1024 * 1024

INPROJ_TILE = 1024
ROW_TILE = 1024
FFN_SUB = 256
ATT_TILE = 256
ATT_QTILE = 256
ATT_DEN_ROWS = 16
ATT_GROUP = 1
ATT_LOOKAHEAD = 5
ATT_EXP_ROWS = 32
ATT_POS_ROWS = V7X_MXU_DIM - DA_QK_PAIR
RET_BLOCK = 128
RET_LOOKAHEAD = 2
LOG2E = math.log2(math.e)
POS_TERMS = 3

_NT = (((1,), (1,)), ((), ()))
_TN = (((0,), (0,)), ((), ()))


def _rms(x):
    return x * lax.rsqrt(jnp.mean(x * x, axis=-1, keepdims=True) + NORM_EPS)


def _silu(x):
    hx = 0.5 * x
    return hx * (jnp.tanh(hx) + 1.0)


def _const_spec(shape):
    nd = len(shape)
    return pl.BlockSpec(shape, lambda *_: (0,) * nd, pipeline_mode=pl.Buffered(1))


def _inproj_kernel(x_ref, g_ref, win_ref, wo32_ref, wg32_ref, wu32_ref, wd32_ref,
                   qT_ref, k_ref, vT_ref, rq_ref, rk_ref, rv_ref, rg_ref,
                   wo_ref, wg_ref, wu_ref, wd_ref,
                   wqT_ref, wk_ref, wvT_ref, wr_ref):
    c1 = DA_Q_COLS
    c2 = c1 + DA_K_COLS
    c3 = c2 + DA_V_COLS

    @pl.when(pl.program_id(0) == 0)
    def _():
        for c in range(0, DA_Q_COLS, V7X_LANES):
            cols = slice(c, c + V7X_LANES)
            wqT_ref[cols, :] = win_ref[:, cols].T.astype(BF16)
            wvT_ref[cols, :] = win_ref[:, c2 + c:c2 + c + V7X_LANES].T.astype(BF16)
        wk_ref[...] = win_ref[:, c1:c2].astype(BF16)
        wr_ref[...] = win_ref[:, c3:].astype(BF16)

    wo_ref[...] = wo32_ref[...].astype(BF16)
    wg_ref[...] = wg32_ref[...].astype(BF16)
    wu_ref[...] = wu32_ref[...].astype(BF16)
    wd_ref[...] = wd32_ref[...].astype(BF16)

    for j in range(INPROJ_TILE // ATT_TILE):
        rows = slice(j * ATT_TILE, (j + 1) * ATT_TILE)
        h = (_rms(x_ref[rows, :]) * g_ref[...]).astype(BF16)
        qT = lax.dot_general(wqT_ref[...], h, _NT, preferred_element_type=F32)
        qT_ref[:, rows] = (qT * (DA_QK_DIM ** -0.5 * LOG2E)).astype(BF16)
        k = jnp.dot(h, wk_ref[...], preferred_element_type=F32)
        for hh in range(DA_HEADS):
            k_ref[hh, rows, :] = k[:, hh * DA_QK_PAIR:(hh + 1) * DA_QK_PAIR].astype(BF16)
        vT = lax.dot_general(wvT_ref[...], h, _NT, preferred_element_type=F32)
        vT_ref[j] = vT.astype(BF16)
        r = jnp.dot(h, wr_ref[...], preferred_element_type=F32)
        r1 = RET_Q_COLS
        r2 = r1 + RET_K_COLS
        r3 = r2 + RET_WIDTH
        rq_ref[rows, :] = r[:, :r1].astype(BF16)
        rk_ref[rows, :] = r[:, r1:r2].astype(BF16)
        rv_ref[rows, :] = r[:, r2:r3].astype(BF16)
        rg_ref[rows, :] = r[:, r3:].astype(BF16)


def _inproj(x2, g, w_in, w_out, w_gate, w_up, w_down):
    n = x2.shape[0]
    tm = INPROJ_TILE
    steps = n // tm
    d_ff = w_gate.shape[1]
    wo_rows, wg_rows, wd_rows = D_MODEL // steps, D_MODEL // steps, d_ff // (steps // 2)
    assert wo_rows * steps == D_MODEL and wd_rows * (steps // 2) == d_ff
    assert wo_rows % V7X_BF16_SUBLANES == 0 and wd_rows % V7X_BF16_SUBLANES == 0
    row = lambda i: (i, 0)
    out_shapes = (
        jax.ShapeDtypeStruct((DA_Q_COLS, n), BF16),
        jax.ShapeDtypeStruct((DA_HEADS, n, DA_QK_PAIR), BF16),
        jax.ShapeDtypeStruct((n // ATT_TILE, DA_V_COLS, ATT_TILE), BF16),
        jax.ShapeDtypeStruct((n, RET_Q_COLS), BF16),
        jax.ShapeDtypeStruct((n, RET_K_COLS), BF16),
        jax.ShapeDtypeStruct((n, RET_WIDTH), BF16),
        jax.ShapeDtypeStruct((n, RET_WIDTH), BF16),
        jax.ShapeDtypeStruct(w_out.shape, BF16),
        jax.ShapeDtypeStruct(w_gate.shape, BF16),
        jax.ShapeDtypeStruct(w_up.shape, BF16),
        jax.ShapeDtypeStruct(w_down.shape, BF16),
    )
    weight_specs = [
        pl.BlockSpec((wo_rows, D_MODEL), row),
        pl.BlockSpec((wg_rows, d_ff), row),
        pl.BlockSpec((wg_rows, d_ff), row),
        pl.BlockSpec((wd_rows, D_MODEL), lambda i: (i // 2, 0)),
    ]
    out_specs = (
        pl.BlockSpec((DA_Q_COLS, tm), lambda i: (0, i)),
        pl.BlockSpec((DA_HEADS, tm, DA_QK_PAIR), lambda i: (0, i, 0)),
        pl.BlockSpec((tm // ATT_TILE, DA_V_COLS, ATT_TILE), lambda i: (i, 0, 0)),
        pl.BlockSpec((tm, RET_Q_COLS), row),
        pl.BlockSpec((tm, RET_K_COLS), row),
        pl.BlockSpec((tm, RET_WIDTH), row),
        pl.BlockSpec((tm, RET_WIDTH), row),
        *weight_specs,
    )
    return pl.pallas_call(
        _inproj_kernel,
        grid=(steps,),
        in_specs=[
            pl.BlockSpec((tm, D_MODEL), row),
            _const_spec((1, D_MODEL)),
            _const_spec(w_in.shape),
            *weight_specs,
        ],
        out_specs=out_specs,
        out_shape=out_shapes,
        scratch_shapes=[
            pltpu.VMEM((DA_Q_COLS, D_MODEL), BF16),
            pltpu.VMEM((D_MODEL, DA_K_COLS), BF16),
            pltpu.VMEM((DA_V_COLS, D_MODEL), BF16),
            pltpu.VMEM((D_MODEL, w_in.shape[1] - DA_Q_COLS - DA_K_COLS - DA_V_COLS), BF16),
        ],
        compiler_params=pltpu.CompilerParams(
            dimension_semantics=("arbitrary",), vmem_limit_bytes=VMEM_LIMIT_BYTES),
        name="inproj",
    )(x2, g, w_in, w_out, w_gate, w_up, w_down)


def _attn_tables():
    t = ATT_TILE
    kk = np.arange(t)[:, None]
    qq = np.arange(t)[None, :]
    allowed = (kk // CHUNK) <= (qq // CHUNK)
    diag = np.empty((DA_HEADS, t, 2 * t), np.float32)
    prow = np.zeros((DA_HEADS, ATT_POS_ROWS, 2 * ATT_QTILE), np.float32)
    for h, slope in enumerate(ALIBI_SLOPES):
        c = slope * LOG2E
        d = np.where(allowed, -2.0 * c * np.maximum(kk - qq, 0), -np.inf)
        diag[h] = np.concatenate([d, d], axis=1)
        rem = c
        for j in range(POS_TERMS):
            piece = float(np.asarray(rem, np.float32).astype(jnp.bfloat16).astype(np.float32))
            prow[h, j, :] = piece * CHUNK
            prow[h, POS_TERMS + j, :] = piece
            rem = rem - piece
    return diag, prow


def _pos_features(s):
    pos = np.arange(s)
    feat = np.zeros((s, ATT_POS_ROWS), np.float32)
    for j in range(POS_TERMS):
        feat[:, j] = pos // CHUNK
        feat[:, POS_TERMS + j] = pos % CHUNK
    return feat


def _attn_kernel(lam_init, qT_ref, k_ref, pos_ref, vT_ref, diag_ref, prow_ref, lamv_ref, g_ref,
                 o_ref, acc_ref, *sp_refs):
    t = ATT_TILE
    tq = ATT_QTILE
    w = 2 * tq
    nq = qT_ref.shape[1] // tq
    s_refs = sp_refs[:ATT_LOOKAHEAD + ATT_GROUP]
    p_refs = sp_refs[ATT_LOOKAHEAD + ATT_GROUP:]
    z = jnp.zeros((DA_QK_DIM, t), BF16)
    ones_rows = (lax.broadcasted_iota(jnp.int32, (ATT_DEN_ROWS, t), 0) == 0).astype(BF16)
    diag = diag_ref[0]
    lv = lamv_ref[...]
    lam = (jnp.exp(jnp.sum(lv[0:1] * lv[1:2], axis=-1, keepdims=True))
           - jnp.exp(jnp.sum(lv[2:3] * lv[3:4], axis=-1, keepdims=True)) + lam_init)

    def keys(k0, nk):
        rows = slice(k0 * t, (k0 + nk) * t)
        return jnp.concatenate([k_ref[rows, :], pos_ref[rows, :]], axis=1)

    def values_t(k0, nk):
        return jnp.concatenate(
            [jnp.concatenate([vT_ref[ki], ones_rows], axis=0) for ki in range(k0, k0 + nk)], axis=1)

    def colmax(a):
        return jnp.max(a, axis=0, keepdims=True)

    def prob(a, m):
        return jnp.exp2(a - m).astype(BF16)

    def combine(a):
        num = a[:DA_V_DIM] * (1.0 / a[DA_V_DIM:DA_V_DIM + 1])
        return num[:, :t] - lam * num[:, t:]

    def query_operand(qi):
        qh = qT_ref[:, qi * tq:(qi + 1) * tq]
        cols = [jnp.concatenate([qh[:DA_QK_DIM], z], axis=0),
                jnp.concatenate([z, qh[DA_QK_DIM:]], axis=0)]
        return jnp.concatenate([jnp.concatenate(cols, axis=1), prow_ref[0]], axis=0)

    tasks = []
    for qi in range(nq):
        tasks.append((qi, "diag", qi))
        tasks += [(qi, "full", k0) for k0 in range(qi)]

    qexts = {}

    def scores(task):
        qi, kind, k0 = task
        if qi not in qexts:
            qexts[qi] = query_operand(qi)
        s = jnp.dot(keys(k0, 1), qexts[qi], preferred_element_type=F32)
        return s + diag if kind == "diag" else s

    row0 = pl.multiple_of(jnp.minimum(pl.program_id(0), 0), t)

    def issue(idx):
        s = scores(tasks[idx])
        s_refs[idx % len(s_refs)][pl.ds(row0, t), :s.shape[1]] = s
        return colmax(s)

    def probabilities(gi, group, m):
        ncol = w
        p_ref = p_refs[gi % len(p_refs)]
        for j, idx in enumerate(group):
            s_ref = s_refs[idx % len(s_refs)]
            for r in range(0, t, ATT_EXP_ROWS):
                rows = pl.ds(pl.multiple_of(row0 + r, ATT_EXP_ROWS), ATT_EXP_ROWS)
                out_rows = pl.ds(pl.multiple_of(row0 + j * t + r, ATT_EXP_ROWS), ATT_EXP_ROWS)
                p_ref[out_rows, :ncol] = prob(s_ref[rows, :ncol], m)
        return p_ref[pl.ds(row0, len(group) * t), :ncol]

    groups = []
    idx = 0
    for qi in range(nq):
        groups.append([idx])
        idx += 1
        for g0 in range(0, qi, ATT_GROUP):
            size = min(ATT_GROUP, qi - g0)
            groups.append(list(range(idx, idx + size)))
            idx += size

    smaxes = []
    m = None
    for gi, group in enumerate(groups):
        while len(smaxes) < min(group[-1] + 1 + ATT_LOOKAHEAD, len(tasks)):
            smaxes.append(issue(len(smaxes)))
        qi, kind, k0 = tasks[group[0]]
        smax = functools.reduce(jnp.maximum, [smaxes[i] for i in group])
        acc = acc_ref.at[qi % 2]
        if kind == "diag":
            m = smax
            acc[...] = jnp.dot(
                values_t(k0, 1), probabilities(gi, group, m), preferred_element_type=F32)
        else:
            m_new = jnp.maximum(m, smax)
            pv = jnp.dot(values_t(k0, len(group)), probabilities(gi, group, m_new),
                         preferred_element_type=F32)
            acc[...] = acc[...] * jnp.exp2(m - m_new) + pv
            m = m_new
        if gi + 1 == len(groups) or tasks[groups[gi + 1][0]][0] != qi:
            o = combine(acc[...])
            o = o * lax.rsqrt(jnp.mean(o * o, axis=0, keepdims=True) + NORM_EPS)
            o_ref[qi * tq:(qi + 1) * tq, :] = (
                o.T * g_ref[...] * (1.0 - lam_init)).astype(o_ref.dtype)


def _attention(qT, k, vT, lamv, subln_g, batch, seq, lam_init):
    t = ATT_TILE
    tq = ATT_QTILE
    nk = seq // t
    n = batch * seq
    diag, prow = _attn_tables()
    diag = jnp.asarray(diag)
    prow = jnp.asarray(prow, BF16)
    pos = jnp.asarray(_pos_features(seq), BF16)
    return pl.pallas_call(
        functools.partial(_attn_kernel, lam_init),
        grid=(batch, DA_HEADS),
        in_specs=[
            pl.BlockSpec((DA_QK_PAIR, seq), lambda b, h: (h, b)),
            pl.BlockSpec((None, seq, DA_QK_PAIR), lambda b, h: (h, b, 0)),
            pl.BlockSpec((seq, ATT_POS_ROWS), lambda b, h: (0, 0)),
            pl.BlockSpec((nk, DA_V_DIM, t), lambda b, h: (b, h, 0)),
            pl.BlockSpec((1, t, 2 * t), lambda b, h: (h, 0, 0)),
            pl.BlockSpec((1, ATT_POS_ROWS, 2 * tq), lambda b, h: (h, 0, 0)),
            pl.BlockSpec((4, DA_QK_DIM), lambda b, h: (0, 0)),
            pl.BlockSpec((1, DA_V_DIM), lambda b, h: (0, 0)),
        ],
        out_specs=pl.BlockSpec((seq, DA_V_DIM), lambda b, h: (b, h)),
        out_shape=jax.ShapeDtypeStruct((n, DA_WIDTH), BF16),
        scratch_shapes=[
            pltpu.VMEM((2, DA_V_DIM + ATT_DEN_ROWS, 2 * tq), F32),
            *[pltpu.VMEM((t, 2 * tq), F32) for _ in range(ATT_LOOKAHEAD + ATT_GROUP)],
            *[pltpu.VMEM((ATT_GROUP * t, 2 * tq), BF16) for _ in range(ATT_LOOKAHEAD + 1)],
        ],
        compiler_params=pltpu.CompilerParams(
            dimension_semantics=("arbitrary", "arbitrary"),
            vmem_limit_bytes=VMEM_LIMIT_BYTES),
        name="diff_attention",
    )(qT, k, pos, vT, diag, prow, lamv, subln_g)


def _ret_tables():
    L = RET_BLOCK
    i = np.arange(L)[:, None].astype(np.float64)
    j = np.arange(L)[None, :].astype(np.float64)
    allowed = (j // CHUNK) <= (i // CHUNK)
    dmat = np.empty((RET_HEADS, L, L), np.float32)
    lanes = np.zeros((RET_HEADS, 1, RET_QK_PAIR), np.float32)
    qdec = np.empty((RET_HEADS, L, RET_V_DIM), np.float32)
    kdec = np.zeros((RET_HEADS, L, RET_QK_PAIR), np.float32)
    cdec = np.empty((RET_HEADS, 1, RET_V_DIM), np.float32)
    lane = np.arange(RET_QK_PAIR)[None, :]
    for h, gamma in enumerate(RET_GAMMA):
        mine = (lane // RET_QK_DIM) == (h % 2)
        dmat[h] = np.where(allowed, gamma ** np.abs(i - j), 0.0) * RET_QK_DIM ** -0.5
        lanes[h] = mine
        qdec[h] = gamma ** (i + 1.0)
        kdec[h] = np.where(mine, gamma ** (L - 1.0 - i), 0.0) * RET_QK_DIM ** -0.5
        cdec[h] = gamma ** L
    return dmat, lanes, qdec, kdec, cdec


def _ret_kernel(q_ref, k_ref, v_ref, gate_ref, dmat_ref, lanes_ref, qdec_ref, kdec_ref, cdec_ref,
                g_ref, o_ref):
    L = RET_BLOCK
    tasks = [(blk, h) for blk in range(q_ref.shape[0] // L) for h in range(RET_HEADS)]
    own_lanes = [jnp.broadcast_to(lanes_ref[h] > 0.0, (L, RET_QK_PAIR)) for h in range(RET_HEADS)]

    def operands(task):
        blk, h = task
        rows = slice(blk * L, (blk + 1) * L)
        pair = slice((h // 2) * RET_QK_PAIR, (h // 2 + 1) * RET_QK_PAIR)
        return rows, slice(h * RET_V_DIM, (h + 1) * RET_V_DIM), q_ref[rows, pair], k_ref[rows, pair]

    def scores(task):
        _, h = task
        _, _, q, k = operands(task)
        qm = jnp.where(own_lanes[h], q, jnp.zeros_like(q))
        s = lax.dot_general(qm, k, _NT, preferred_element_type=F32) * dmat_ref[h]
        return s.astype(BF16)

    states = [jnp.zeros((RET_QK_PAIR, RET_V_DIM), F32) for _ in range(RET_HEADS)]
    pending = [scores(task) for task in tasks[:RET_LOOKAHEAD]]
    for idx, task in enumerate(tasks):
        _, h = task
        s = pending.pop(0)
        if idx + RET_LOOKAHEAD < len(tasks):
            pending.append(scores(tasks[idx + RET_LOOKAHEAD]))
        rows, head, q, k = operands(task)
        v = v_ref[rows, head]
        inner = jnp.dot(s, v, preferred_element_type=F32)
        cross = jnp.dot(q, states[h].astype(BF16), preferred_element_type=F32) * qdec_ref[h]
        kd = (k.astype(F32) * kdec_ref[h]).astype(BF16)
        states[h] = states[h] * cdec_ref[h] + lax.dot_general(
            kd, v, _TN, preferred_element_type=F32)
        o = _rms(inner + cross) * g_ref[...]
        o_ref[rows, head] = (_silu(gate_ref[rows, head].astype(F32)) * o).astype(o_ref.dtype)


def _retention(rq, rk, rv, rg, norm_g, batch, seq):
    n = batch * seq
    tables = [jnp.asarray(a) for a in _ret_tables()]
    row = lambda b: (b, 0)
    return pl.pallas_call(
        _ret_kernel,
        grid=(batch,),
        in_specs=[
            pl.BlockSpec((seq, RET_Q_COLS), row),
            pl.BlockSpec((seq, RET_K_COLS), row),
            pl.BlockSpec((seq, RET_WIDTH), row),
            pl.BlockSpec((seq, RET_WIDTH), row),
            *[_const_spec(a.shape) for a in tables],
            _const_spec((1, RET_V_DIM)),
        ],
        out_specs=pl.BlockSpec((seq, RET_WIDTH), row),
        out_shape=jax.ShapeDtypeStruct((n, RET_WIDTH), BF16),
        compiler_params=pltpu.CompilerParams(
            dimension_semantics=("arbitrary",), vmem_limit_bytes=VMEM_LIMIT_BYTES),
        name="retention",
    )(rq, rk, rv, rg, *tables, norm_g)


def _ffn_kernel(x_ref, oa_ref, or_ref, wo_ref, g1_ref, g2_ref, wg_ref, wu_ref, wd_ref, g3_ref,
                out_ref):
    subs = [slice(j * FFN_SUB, (j + 1) * FFN_SUB) for j in range(x_ref.shape[0] // FFN_SUB)]
    mix = [jnp.dot(oa_ref[r, :], wo_ref[:DA_WIDTH, :], preferred_element_type=F32)
           + jnp.dot(or_ref[r, :], wo_ref[DA_WIDTH:, :], preferred_element_type=F32) for r in subs]
    x1 = [x_ref[r, :] + _rms(mx) * g1_ref[...] for r, mx in zip(subs, mix)]
    h = [(_rms(a) * g2_ref[...]).astype(BF16) for a in x1]
    f = []
    y = []
    for a in h:
        gt = jnp.dot(a, wg_ref[...], preferred_element_type=F32)
        u = jnp.dot(a, wu_ref[...], preferred_element_type=F32)
        if f:
            y.append(jnp.dot(f[-1], wd_ref[...], preferred_element_type=F32))
        f.append((_silu(gt) * u).astype(BF16))
    y.append(jnp.dot(f[-1], wd_ref[...], preferred_element_type=F32))
    for r, a, b in zip(subs, x1, y):
        out_ref[r, :] = a + _rms(b) * g3_ref[...]


def _ffn(x2, oa, orr, wo, g1, g2, wg, wu, wd, g3):
    n = x2.shape[0]
    tm = ROW_TILE
    row = lambda i: (i, 0)
    return pl.pallas_call(
        _ffn_kernel,
        grid=(n // tm,),
        in_specs=[
            pl.BlockSpec((tm, D_MODEL), row),
            pl.BlockSpec((tm, DA_WIDTH), row),
            pl.BlockSpec((tm, RET_WIDTH), row),
            _const_spec(wo.shape),
            _const_spec((1, D_MODEL)),
            _const_spec((1, D_MODEL)),
            _const_spec(wg.shape),
            _const_spec(wu.shape),
            _const_spec(wd.shape),
            _const_spec((1, D_MODEL)),
        ],
        out_specs=pl.BlockSpec((tm, D_MODEL), row),
        out_shape=jax.ShapeDtypeStruct((n, D_MODEL), F32),
        compiler_params=pltpu.CompilerParams(
            dimension_semantics=("arbitrary",), vmem_limit_bytes=VMEM_LIMIT_BYTES),
        name="outproj_ffn",
    )(x2, oa, orr, wo, g1, g2, wg, wu, wd, g3)


def kernel(x, pre_mix_g, w_in, lambda_q1, lambda_k1, lambda_q2, lambda_k2, da_subln_g, ret_norm_g,
           w_out, post_mix_g, pre_ffn_g, w_gate, w_up, w_down, post_ffn_g):
    batch, seq, d = x.shape
    depth = w_in.shape[0]
    assert d == D_MODEL and seq % ATT_QTILE == 0 and seq % RET_BLOCK == 0
    assert ATT_QTILE == ATT_TILE and ATT_TILE % CHUNK == 0 and RET_BLOCK % CHUNK == 0
    assert DA_QK_PAIR == V7X_LANES and RET_QK_PAIR == V7X_LANES
    assert DA_V_DIM == V7X_LANES and RET_V_DIM == V7X_LANES and ATT_TILE == V7X_MXU_DIM
    assert (batch * seq) % ROW_TILE == 0 and ROW_TILE % FFN_SUB == 0
    assert (batch * seq) % INPROJ_TILE == 0 and INPROJ_TILE % ATT_TILE == 0
    x2 = x.reshape(batch * seq, d)
    for l in range(depth):
        qT, k, vT, rq, rk, rv, rg, wo, wg, wu, wd = _inproj(
            x2, pre_mix_g[l][None, :], w_in[l], w_out[l], w_gate[l], w_up[l], w_down[l])
        lam_init = 0.8 - 0.6 * math.exp(-0.3 * l)
        lamv = jnp.stack([lambda_q1[l], lambda_k1[l], lambda_q2[l], lambda_k2[l]]).astype(F32)
        o_da = _attention(qT, k, vT, lamv, da_subln_g[l][None, :].astype(F32), batch, seq, lam_init)
        o_ret = _retention(rq, rk, rv, rg, ret_norm_g[l][None, :].astype(F32), batch, seq)
        x2 = _ffn(x2, o_da, o_ret, wo, post_mix_g[l][None, :], pre_ffn_g[l][None, :],
                  wg, wu, wd, post_ffn_g[l][None, :])
    return x2.reshape(batch, seq, d)
```
